```python
import jax, jax.numpy as jnp
from jax import lax
import numpy as np

D_MODEL = 1024
BATCH = 4
SEQ = 4096
DEPTH = 2

GRID_W = 64
CTX_LEN = 256

RWKV_HEADS = 8
RWKV_HEAD_DIM = 64
RWKV_WIDTH = RWKV_HEADS * RWKV_HEAD_DIM
DECAY_LORA = 64
AAA_LORA = 64
GATE_LORA = 128
N_DIR = 2
GN_EPS = 64e-5

MLA_HEADS = 8
QK_NOPE = 64
QK_ROPE = 32
QK_HEAD = QK_NOPE + QK_ROPE
V_HEAD = 64
MLA_WIDTH = MLA_HEADS * V_HEAD
Q_LORA = 384
KV_LORA = 128
ROPE_AXIS = QK_ROPE // 2
ROPE_THETA = 10000.0
Q_BLOCK = 128

FNO_GROUPS = 8
FNO_GROUP_DIM = 64
FNO_WIDTH = FNO_GROUPS * FNO_GROUP_DIM

N_BRANCH = 3
N_RWKV_IN = 3 * RWKV_WIDTH + N_DIR * DECAY_LORA + N_DIR * AAA_LORA + GATE_LORA
N_MLA_IN = Q_LORA + KV_LORA + QK_ROPE
N_IN = N_RWKV_IN + N_MLA_IN + FNO_WIDTH + N_BRANCH * D_MODEL
IN_SPLITS = (N_RWKV_IN, N_RWKV_IN + N_MLA_IN, N_RWKV_IN + N_MLA_IN + FNO_WIDTH)
RWKV_SPLITS = (RWKV_WIDTH, 2 * RWKV_WIDTH, 3 * RWKV_WIDTH,
               3 * RWKV_WIDTH + N_DIR * DECAY_LORA,
               3 * RWKV_WIDTH + N_DIR * DECAY_LORA + N_DIR * AAA_LORA)

FFN_HIDDEN = ((8 * D_MODEL + 3 * 256 - 1) // (3 * 256)) * 256
RMS_EPS = 1e-6

kernel_name = "hybrid_rwkv7_mla_fnet_dit_block"


def rms_norm(x, g):
    xf = x.astype(jnp.float32)
    y = xf * lax.rsqrt(jnp.mean(xf * xf, axis=-1, keepdims=True) + RMS_EPS)
    return (y * g.astype(jnp.float32)).astype(x.dtype)


def modulate(h, shift, scale):
    return h * (1.0 + scale) + shift


def centred_shift_mix(z, mu_prev, mu_next):
    z_prev = jnp.pad(z, ((0, 0), (1, 0), (0, 0)))[:, :-1]
    z_next = jnp.pad(z, ((0, 0), (0, 1), (0, 0)))[:, 1:]
    return z + (z_prev - z) * mu_prev + (z_next - z) * mu_next


def rwkv_features(z, p):
    z = centred_shift_mix(z, p["mu_prev"], p["mu_next"])
    B, T, _ = z.shape
    r, k, v, zw, za, zg = jnp.split(z, RWKV_SPLITS, axis=-1)
    heads = lambda t: t.reshape(B, T, RWKV_HEADS, RWKV_HEAD_DIM)
    dir_heads = lambda t: t.reshape(B, T, N_DIR, RWKV_HEADS, RWKV_HEAD_DIM)
    zw = jnp.tanh(zw.reshape(B, T, N_DIR, DECAY_LORA))
    w_log = -jax.nn.softplus(-(p["w0"] + jnp.einsum("btdr,drc->btdc", zw, p["w2"]))) - 0.5
    decay = jnp.exp(-jnp.exp(w_log.astype(jnp.float32)))
    a = jax.nn.sigmoid(p["a0"] + jnp.einsum("btdr,drc->btdc", za.reshape(B, T, N_DIR, AAA_LORA), p["a2"]))
    g = jax.nn.sigmoid(zg) @ p["g2"]
    kk = heads(k * p["k_k"]).astype(jnp.float32)
    kk = kk / jnp.maximum(jnp.sqrt(jnp.sum(kk * kk, axis=-1, keepdims=True)), 1e-12)
    k_dir = k[:, :, None, :] * (1.0 + (a - 1.0) * p["k_a"])
    b_dir = kk[:, :, None] * dir_heads(a).astype(jnp.float32)
    return dict(r=heads(r), v=heads(v), kk=kk, g=g, decay=dir_heads(decay), k=dir_heads(k_dir), b=b_dir)


def rwkv_scan(state0, f, d, reverse):
    seq = (f["r"], f["decay"][:, :, d], f["k"][:, :, d], f["v"], f["kk"], f["b"][:, :, d])
    xs = tuple(jnp.moveaxis(t.astype(jnp.float32), 1, 0) for t in seq)

    def step(S, inp):
        r_t, w_t, k_t, v_t, kk_t, b_t = inp
        sa = -jnp.einsum("bhvk,bhk->bhv", S, kk_t)
        S = S * w_t[:, :, None, :] + sa[..., None] * b_t[:, :, None, :] + v_t[..., None] * k_t[:, :, None, :]
        return S, jnp.einsum("bhvk,bhk->bhv", S, r_t)

    S, ys = lax.scan(step, state0, xs, reverse=reverse)
    return S, jnp.moveaxis(ys, 0, 1)


def rwkv_output(y, f, p):
    B, T = y.shape[:2]
    mu = jnp.mean(y, axis=-1, keepdims=True)
    var = jnp.mean(jnp.square(y - mu), axis=-1, keepdims=True)
    yn = ((y - mu) * lax.rsqrt(var + GN_EPS)).reshape(B, T, RWKV_WIDTH) * p["gn_w"] + p["gn_b"]
    k_sum = f["k"].sum(axis=2)
    bonus = jnp.sum(f["r"] * p["r_k"] * k_sum, axis=-1, keepdims=True) * f["v"]
    o = (yn + bonus.reshape(B, T, RWKV_WIDTH)) * f["g"]
    return o.astype(f["g"].dtype) @ p["w_rwkv_o"]


def axial_rope_angles(n_tokens):
    rows = n_tokens // GRID_W
    row = jnp.broadcast_to(jnp.arange(rows)[:, None], (rows, GRID_W)).reshape(-1).astype(jnp.float32)
    col = jnp.broadcast_to(jnp.arange(GRID_W)[None, :], (rows, GRID_W)).reshape(-1).astype(jnp.float32)
    inv_freq = ROPE_THETA ** (-jnp.arange(0, ROPE_AXIS, 2, dtype=jnp.float32) / ROPE_AXIS)
    return row[:, None] * inv_freq, col[:, None] * inv_freq


def rotate_half(t, ang):
    t1, t2 = jnp.split(t, 2, axis=-1)
    cos = jnp.cos(ang)[:, None, :].astype(t.dtype)
    sin = jnp.sin(ang)[:, None, :].astype(t.dtype)
    return jnp.concatenate([t1 * cos - t2 * sin, t2 * cos + t1 * sin], axis=-1)


def apply_axial_rope(t, angles):
    ang_row, ang_col = angles
    t_nope, t_row, t_col = jnp.split(t, (QK_NOPE, QK_NOPE + ROPE_AXIS), axis=-1)
    return jnp.concatenate([t_nope, rotate_half(t_row, ang_row), rotate_half(t_col, ang_col)], axis=-1)


def mla_qkv(z, p, angles):
    B, T, _ = z.shape
    c_q, c_kv, k_rope = jnp.split(z, (Q_LORA, Q_LORA + KV_LORA), axis=-1)
    q = (rms_norm(c_q, p["g_cq"]) @ p["w_uq"]).reshape(B, T, MLA_HEADS, QK_HEAD)
    kv = (rms_norm(c_kv, p["g_ckv"]) @ p["w_ukv"]).reshape(B, T, MLA_HEADS, QK_NOPE + V_HEAD)
    k_nope, v = jnp.split(kv, (QK_NOPE,), axis=-1)
    k = jnp.concatenate([k_nope, jnp.broadcast_to(k_rope[:, :, None, :], (B, T, MLA_HEADS, QK_ROPE))], axis=-1)
    q = rms_norm(q, p["g_qn"])
    k = rms_norm(k, p["g_kn"])
    if angles is not None:
        q = apply_axial_rope(q, angles)
        k = apply_axial_rope(k, angles)
    return q, k, v


def block_attention(q, k, v):
    B, T, H, Dk = q.shape
    scale = Dk ** -0.5
    qb = q.reshape(B, T // Q_BLOCK, Q_BLOCK, H, Dk).transpose(1, 0, 2, 3, 4)

    def one_block(q_blk):
        s = jnp.einsum("bqhd,bkhd->bhqk", q_blk, k).astype(jnp.float32) * scale
        pr = jax.nn.softmax(s, axis=-1)
        return jnp.einsum("bhqk,bkhd->bqhd", pr.astype(v.dtype), v)

    o = lax.map(one_block, qb)
    return o.transpose(1, 0, 2, 3, 4).reshape(B, T, H * v.shape[-1])


def fourier_mix(z):
    B, T, _ = z.shape
    zg = z.astype(jnp.float32).reshape(B, T, FNO_GROUPS, FNO_GROUP_DIM)
    y = jnp.fft.fft2(zg, axes=(1, 3), norm="ortho").real
    return y.reshape(B, T, FNO_WIDTH).astype(z.dtype)


def merge_branches(z_gate, a_o, b_o, c_o, w_out):
    B, T, _ = z_gate.shape
    g = jax.nn.sigmoid(z_gate.reshape(B, T, N_BRANCH, D_MODEL))
    return (g[:, :, 0] * a_o + g[:, :, 1] * b_o + g[:, :, 2] * c_o) @ w_out


def swiglu(h, w_in, w_out):
    gate, up = jnp.split(h @ w_in, 2, axis=-1)
    return (jax.nn.silu(gate) * up) @ w_out


def setup_inputs(seed: int = 0) -> dict:
    key = jax.random.key(seed)
    ks = iter(jax.random.split(key, 64))
    L, D = DEPTH, D_MODEL
    f32 = jnp.float32
    nrm = lambda shape, s: jax.random.normal(next(ks), shape, f32) * s
    uni = lambda shape, lo, hi: jax.random.uniform(next(ks), shape, f32, lo, hi)
    return {
        "x": nrm((BATCH, SEQ, D), 1.0),
        "c": nrm((BATCH, D), 1.0),
        "ctx": nrm((BATCH, CTX_LEN, D), 1.0),
        "c_ctx": nrm((D,), 1.0),
        "w_mod": nrm((L, D, 6 * D), 0.02),
        "b_mod": nrm((L, 6 * D), 0.01),
        "g_norm1": 1.0 + nrm((L, D), 0.02),
        "w_in": nrm((L, D, N_IN), D ** -0.5),
        "mu_prev": uni((L, N_RWKV_IN), 0.0, 0.5),
        "mu_next": uni((L, N_RWKV_IN), 0.0, 0.5),
        "w0": uni((L, N_DIR, RWKV_WIDTH), -5.0, 1.0),
        "w2": nrm((L, N_DIR, DECAY_LORA, RWKV_WIDTH), 0.1 * DECAY_LORA ** -0.5),
        "a0": nrm((L, N_DIR, RWKV_WIDTH), 0.1),
        "a2": nrm((L, N_DIR, AAA_LORA, RWKV_WIDTH), 0.5 * AAA_LORA ** -0.5),
        "k_k": 0.85 + nrm((L, RWKV_WIDTH), 0.02),
        "k_a": 1.0 + nrm((L, RWKV_WIDTH), 0.02),
        "r_k": nrm((L, RWKV_HEADS, RWKV_HEAD_DIM), 0.1),
        "g2": nrm((L, GATE_LORA, RWKV_WIDTH), GATE_LORA ** -0.5),
        "gn_w": 1.0 + nrm((L, RWKV_WIDTH), 0.02),
        "gn_b": nrm((L, RWKV_WIDTH), 0.02),
        "w_rwkv_o": nrm((L, RWKV_WIDTH, D), RWKV_WIDTH ** -0.5),
        "g_cq": 1.0 + nrm((L, Q_LORA), 0.02),
        "g_ckv": 1.0 + nrm((L, KV_LORA), 0.02),
        "w_uq": nrm((L, Q_LORA, MLA_HEADS * QK_HEAD), Q_LORA ** -0.5),
        "w_ukv": nrm((L, KV_LORA, MLA_HEADS * (QK_NOPE + V_HEAD)), KV_LORA ** -0.5),
        "g_qn": 1.0 + nrm((L, QK_HEAD), 0.02),
        "g_kn": 1.0 + nrm((L, QK_HEAD), 0.02),
        "w_mla_o": nrm((L, MLA_WIDTH, D), MLA_WIDTH ** -0.5),
        "w_fno": nrm((L, FNO_WIDTH, D), FNO_WIDTH ** -0.5),
        "w_out": nrm((L, D, D), D ** -0.5),
        "g_norm2": 1.0 + nrm((L, D), 0.02),
        "w_ffn_in": nrm((L, D, 2 * FFN_HIDDEN), D ** -0.5),
        "w_ffn_out": nrm((L, FFN_HIDDEN, D), FFN_HIDDEN ** -0.5),
    }


def reference(x, c, ctx, c_ctx, w_mod, b_mod, g_norm1, w_in, mu_prev, mu_next, w0, w2, a0, a2,
              k_k, k_a, r_k, g2, gn_w, gn_b, w_rwkv_o, g_cq, g_ckv, w_uq, w_ukv, g_qn, g_kn,
              w_mla_o, w_fno, w_out, g_norm2, w_ffn_in, w_ffn_out):
    layer_params = dict(w_mod=w_mod, b_mod=b_mod, g_norm1=g_norm1, w_in=w_in, mu_prev=mu_prev,
                        mu_next=mu_next, w0=w0, w2=w2, a0=a0, a2=a2, k_k=k_k, k_a=k_a, r_k=r_k,
                        g2=g2, gn_w=gn_w, gn_b=gn_b, w_rwkv_o=w_rwkv_o, g_cq=g_cq, g_ckv=g_ckv,
                        w_uq=w_uq, w_ukv=w_ukv, g_qn=g_qn, g_kn=g_kn, w_mla_o=w_mla_o, w_fno=w_fno,
                        w_out=w_out, g_norm2=g_norm2, w_ffn_in=w_ffn_in, w_ffn_out=w_ffn_out)
    B, S, _ = x.shape
    angles = axial_rope_angles(S)
    zero_state = jnp.zeros((B, RWKV_HEADS, RWKV_HEAD_DIM, RWKV_HEAD_DIM), jnp.float32)
    cs = ctx
    for layer in range(DEPTH):
        p = {name: arr[layer] for name, arr in layer_params.items()}
        last = layer == DEPTH - 1
        mod_x = [m[:, None, :] for m in jnp.split(jax.nn.silu(c) @ p["w_mod"] + p["b_mod"], 6, axis=-1)]
        mod_c = jnp.split(jax.nn.silu(c_ctx) @ p["w_mod"] + p["b_mod"], 6, axis=-1)

        hc = modulate(rms_norm(cs, p["g_norm1"]), mod_c[0], mod_c[1])
        hx = modulate(rms_norm(x, p["g_norm1"]), mod_x[0], mod_x[1])
        zr_c, zm_c, zf_c, zg_c = jnp.split(hc @ p["w_in"], IN_SPLITS, axis=-1)
        zr_x, zm_x, zf_x, zg_x = jnp.split(hx @ p["w_in"], IN_SPLITS, axis=-1)

        fc = rwkv_features(zr_c, p)
        fx = rwkv_features(zr_x, p)
        s_fwd_c, y_fwd_c = rwkv_scan(zero_state, fc, 0, reverse=False)
        s_bwd_c, y_bwd_c = rwkv_scan(zero_state, fc, 1, reverse=True)
        _, y_fwd_x = rwkv_scan(s_fwd_c, fx, 0, reverse=False)
        _, y_bwd_x = rwkv_scan(s_bwd_c, fx, 1, reverse=True)
        a_x = rwkv_output(y_fwd_x + y_bwd_x, fx, p)

        q_c, k_c, v_c = mla_qkv(zm_c, p, None)
        q_x, k_x, v_x = mla_qkv(zm_x, p, angles)
        b_x = block_attention(q_x, jnp.concatenate([k_c, k_x], axis=1),
                              jnp.concatenate([v_c, v_x], axis=1)) @ p["w_mla_o"]

        f_x = fourier_mix(zf_x) @ p["w_fno"]

        x_mid = x + mod_x[2] * merge_branches(zg_x, a_x, b_x, f_x, p["w_out"])
        hx2 = modulate(rms_norm(x_mid, p["g_norm2"]), mod_x[3], mod_x[4])
        x_new = x_mid + mod_x[5] * swiglu(hx2, p["w_ffn_in"], p["w_ffn_out"])

        if not last:
            a_c = rwkv_output(y_fwd_c + y_bwd_c, fc, p)
            b_c = block_attention(q_c, k_c, v_c) @ p["w_mla_o"]
            f_c = fourier_mix(zf_c) @ p["w_fno"]
            c_mid = cs + mod_c[2] * merge_branches(zg_c, a_c, b_c, f_c, p["w_out"])
            hc2 = modulate(rms_norm(c_mid, p["g_norm2"]), mod_c[3], mod_c[4])
            cs = c_mid + mod_c[5] * swiglu(hc2, p["w_ffn_in"], p["w_ffn_out"])
        x = x_new
    return x
```

```python
import functools
import math

import jax
import jax.numpy as jnp
from jax import lax
from jax.experimental import pallas as pl
from jax.experimental.pallas import tpu as pltpu

F32 = jnp.float32
BF16 = jnp.bfloat16

D_MODEL = 1024
GRID_W = 64
RWKV_HEADS = 8
RWKV_HEAD_DIM = 64
RWKV_WIDTH = RWKV_HEADS * RWKV_HEAD_DIM
DECAY_LORA = 64
AAA_LORA = 64
GATE_LORA = 128
N_DIR = 2
GN_EPS = 64e-5
MLA_HEADS = 8
QK_NOPE = 64
QK_ROPE = 32
QK_HEAD = QK_NOPE + QK_ROPE
V_HEAD = 64
Q_LORA = 384
KV_LORA = 128
ROPE_AXIS = QK_ROPE // 2
ROPE_THETA = 10000.0
FNO_GROUPS = 8
FNO_GROUP_DIM = 64
FNO_WIDTH = FNO_GROUPS * FNO_GROUP_DIM
N_RWKV_IN = 3 * RWKV_WIDTH + N_DIR * DECAY_LORA + N_DIR * AAA_LORA + GATE_LORA
N_MLA_IN = Q_LORA + KV_LORA + QK_ROPE
RMS_EPS = 1e-6

CHUNK = 64
QK_PAD = 128
VMEM_LIMIT = 48 * 1024 * 1024


def _mm_body(a_ref, b_ref, o_ref, *, scale):
    acc = jnp.dot(a_ref[...], b_ref[...], preferred_element_type=F32)
    if scale != 1.0:
        acc = acc * scale
    o_ref[...] = acc.astype(o_ref.dtype)


def _mm_acc_body(a_ref, b_ref, o_ref, acc_ref, *, nk, scale):
    k = pl.program_id(2)

    @pl.when(k == 0)
    def _():
        acc_ref[...] = jnp.zeros_like(acc_ref)

    acc_ref[...] += jnp.dot(a_ref[...], b_ref[...], preferred_element_type=F32)

    @pl.when(k == nk - 1)
    def _():
        acc = acc_ref[...]
        if scale != 1.0:
            acc = acc * scale
        o_ref[...] = acc.astype(o_ref.dtype)


def _pick(dim, pref):
    t = min(dim, pref)
    while dim % t:
        t //= 2
    return t


def matmul(a, b, out_dtype=F32, tm=1024, tn=1024, tk=4096, scale=1.0):
    a = a.astype(BF16)
    b = b.astype(BF16)
    M, K = a.shape
    _, N = b.shape
    tm, tn, tk = _pick(M, tm), _pick(N, tn), _pick(K, tk)
    nk = K // tk
    params = pltpu.CompilerParams(
        dimension_semantics=("parallel", "parallel") + (("arbitrary",) if nk > 1 else ()),
        vmem_limit_bytes=VMEM_LIMIT)
    if nk == 1:
        return pl.pallas_call(
            functools.partial(_mm_body, scale=scale),
            grid=(M // tm, N // tn),
            in_specs=[pl.BlockSpec((tm, K), lambda i, j: (i, 0)),
                      pl.BlockSpec((K, tn), lambda i, j: (0, j))],
            out_specs=pl.BlockSpec((tm, tn), lambda i, j: (i, j)),
            out_shape=jax.ShapeDtypeStruct((M, N), out_dtype),
            compiler_params=params,
        )(a, b)
    return pl.pallas_call(
        functools.partial(_mm_acc_body, nk=nk, scale=scale),
        grid=(M // tm, N // tn, nk),
        in_specs=[pl.BlockSpec((tm, tk), lambda i, j, k: (i, k)),
                  pl.BlockSpec((tk, tn), lambda i, j, k: (k, j))],
        out_specs=pl.BlockSpec((tm, tn), lambda i, j, k: (i, j)),
        out_shape=jax.ShapeDtypeStruct((M, N), out_dtype),
        scratch_shapes=[pltpu.VMEM((tm, tn), F32)],
        compiler_params=params,
    )(a, b)


def _mm_f32_body(a_ref, b_ref, bias_ref, o_ref):
    o_ref[...] = jnp.dot(a_ref[...], b_ref[...], preferred_element_type=F32,
                         precision=lax.Precision.HIGHEST) + bias_ref[...]


def matmul_f32_bias(a, b, bias, tn=512):
    M, K = a.shape
    _, N = b.shape
    return pl.pallas_call(
        _mm_f32_body,
        grid=(N // tn,),
        in_specs=[pl.BlockSpec((M, K), lambda j: (0, 0)),
                  pl.BlockSpec((K, tn), lambda j: (0, j)),
                  pl.BlockSpec((1, tn), lambda j: (0, j))],
        out_specs=pl.BlockSpec((M, tn), lambda j: (0, j)),
        out_shape=jax.ShapeDtypeStruct((M, N), F32),
        compiler_params=pltpu.CompilerParams(dimension_semantics=("parallel",),
                                             vmem_limit_bytes=VMEM_LIMIT),
    )(a, b, bias.reshape(1, N))


def _dft_body(w_ref, z_ref, o_ref, acc_ref, *, nk, scale):
    k = pl.program_id(2)

    @pl.when(k == 0)
    def _():
        acc_ref[...] = jnp.zeros_like(acc_ref)

    acc_ref[...] += jnp.dot(w_ref[...], z_ref[...], preferred_element_type=F32)

    @pl.when(k == nk - 1)
    def _():
        o_ref[...] = (acc_ref[...] * scale).astype(o_ref.dtype)


def time_dft(w, z, scale, tm=1024, tk=1024):
    T = w.shape[0]
    B, _, W2 = z.shape
    Wd = W2 // 2
    tm, tk = _pick(T, tm), _pick(T, tk)
    nkh = T // tk
    nk = 2 * nkh
    return pl.pallas_call(
        functools.partial(_dft_body, nk=nk, scale=scale),
        grid=(B, T // tm, nk),
        in_specs=[pl.BlockSpec((tm, tk), lambda b, i, k: (i, k)),
                  pl.BlockSpec((None, tk, Wd), lambda b, i, k: (b, k % nkh, k // nkh))],
        out_specs=pl.BlockSpec((None, tm, Wd), lambda b, i, k: (b, i, 0)),
        out_shape=jax.ShapeDtypeStruct((B, T, Wd), BF16),
        scratch_shapes=[pltpu.VMEM((tm, Wd), F32)],
        compiler_params=pltpu.CompilerParams(
            dimension_semantics=("parallel", "parallel", "arbitrary"),
            vmem_limit_bytes=VMEM_LIMIT),
    )(w, z)


def _bdot(a, b):
    return jnp.dot(a.astype(BF16), b.astype(BF16), preferred_element_type=F32)


def _bdot_nt(a, b):
    return lax.dot_general(a.astype(BF16), b.astype(BF16), (((1,), (1,)), ((), ())),
                           preferred_element_type=F32)


def _bdot_tn(a, b):
    return lax.dot_general(a.astype(BF16), b.astype(BF16), (((0,), (0,)), ((), ())),
                           preferred_element_type=F32)


def _rwkv_chunk(S, r, lw, k, v, kk, b, incl, strict, inclf):
    C = CHUNK
    L = jnp.dot(inclf, lw, preferred_element_type=F32, precision=lax.Precision.HIGHEST)
    ltot = L[C - 1:C, :]
    eL = jnp.exp(L)
    eE = jnp.exp(L - lw)
    enL = jnp.exp(-L)
    eT = jnp.exp(ltot - L)
    ah = -kk * eE
    rh = r * eL
    bh = b * enL
    kh = k * enL
    bb = b * eT
    kb = k * eT
    a4 = _bdot_nt(jnp.concatenate([ah, rh], axis=0), jnp.concatenate([bh, kh], axis=0))
    a_ab = jnp.where(strict, a4[:C, :C], 0.0)
    a_ak = jnp.where(strict, a4[:C, C:], 0.0)
    a_rb = jnp.where(incl, a4[C:, :C], 0.0)
    a_rk = jnp.where(incl, a4[C:, C:], 0.0)
    x = a_ab
    pw = a_ab
    for _ in range(int(math.log2(C)) - 1):
        pw = _bdot(pw, pw)
        x = x + pw + _bdot(x, pw)
    x1 = _bdot_nt(ah, S) + _bdot(a_ak, v)
    u = x1 + _bdot(x, x1)
    y = _bdot_nt(rh, S) + _bdot(a_rb, u) + _bdot(a_rk, v)
    s_new = S * jnp.exp(ltot) + _bdot_tn(jnp.concatenate([u, v], axis=0),
                                         jnp.concatenate([bb, kb], axis=0))
    return y, s_new


def _rwkv_body(r_ref, lw_ref, k_ref, v_ref, kk_ref, b_ref, y_ref, s_ref, *, heads):
    c = pl.program_id(1)

    @pl.when(c == 0)
    def _():
        s_ref[...] = jnp.zeros_like(s_ref)

    C = CHUNK
    ti = lax.broadcasted_iota(jnp.int32, (C, C), 0)
    ii = lax.broadcasted_iota(jnp.int32, (C, C), 1)
    incl = ii <= ti
    strict = ii < ti
    inclf = incl.astype(F32)
    for g in range(heads):
        y, s_new = _rwkv_chunk(s_ref[g], r_ref[g], lw_ref[g], k_ref[g], v_ref[g], kk_ref[g],
                               b_ref[g], incl, strict, inclf)
        y_ref[g] = y
        s_ref[g] = s_new


def rwkv_scan(r, lw, k, v, kk, b, heads=4):
    NH, T, N = r.shape
    spec = pl.BlockSpec((heads, CHUNK, N), lambda h, c: (h, c, 0))
    return pl.pallas_call(
        functools.partial(_rwkv_body, heads=heads),
        grid=(NH // heads, T // CHUNK),
        in_specs=[spec] * 6,
        out_specs=spec,
        out_shape=jax.ShapeDtypeStruct((NH, T, N), F32),
        scratch_shapes=[pltpu.VMEM((heads, N, N), F32)],
        compiler_params=pltpu.CompilerParams(
            dimension_semantics=("parallel", "arbitrary"),
            vmem_limit_bytes=VMEM_LIMIT),
    )(r, lw, k, v, kk, b)


def _attn_body(q_ref, k_ref, v_ref, o_ref):
    s = lax.dot_general(q_ref[...], k_ref[...], (((1,), (1,)), ((), ())),
                        preferred_element_type=F32)
    m = jnp.max(s, axis=-1, keepdims=True)
    p = jnp.exp(s - m)
    l = jnp.sum(p, axis=-1, keepdims=True)
    o = jnp.dot(p.astype(BF16), v_ref[...], preferred_element_type=F32)
    o_ref[...] = o / l


def attention(q, k, v, tq=256):
    B, H, T, Dk = q.shape
    N = k.shape[2]
    Dv = v.shape[3]
    tq = _pick(T, tq)
    return pl.pallas_call(
        _attn_body,
        grid=(B, H, T // tq),
        in_specs=[pl.BlockSpec((None, None, tq, Dk), lambda b, h, i: (b, h, i, 0)),
                  pl.BlockSpec((None, None, N, Dk), lambda b, h, i: (b, h, 0, 0)),
                  pl.BlockSpec((None, None, N, Dv), lambda b, h, i: (b, h, 0, 0))],
        out_specs=pl.BlockSpec((None, None, tq, Dv), lambda b, h, i: (b, h, i, 0)),
        out_shape=jax.ShapeDtypeStruct((B, H, T, Dv), F32),
        compiler_params=pltpu.CompilerParams(
            dimension_semantics=("parallel", "parallel", "parallel"),
            vmem_limit_bytes=VMEM_LIMIT),
    )(q, k, v)


def _rms_norm(x, g):
    y = x * lax.rsqrt(jnp.mean(x * x, axis=-1, keepdims=True) + RMS_EPS)
    return y * g


def _shift_mix(z, mu_prev, mu_next):
    z_prev = jnp.pad(z, ((0, 0), (1, 0), (0, 0)))[:, :-1]
    z_next = jnp.pad(z, ((0, 0), (0, 1), (0, 0)))[:, 1:]
    return z + (z_prev - z) * mu_prev + (z_next - z) * mu_next


def _rope_tables(n_tokens):
    rows = n_tokens // GRID_W
    row = jnp.broadcast_to(jnp.arange(rows)[:, None], (rows, GRID_W)).reshape(-1).astype(F32)
    col = jnp.broadcast_to(jnp.arange(GRID_W)[None, :], (rows, GRID_W)).reshape(-1).astype(F32)
    inv_freq = ROPE_THETA ** (-jnp.arange(0, ROPE_AXIS, 2, dtype=F32) / ROPE_AXIS)
    return row[:, None] * inv_freq, col[:, None] * inv_freq


def _rotate_half(t, ang):
    t1, t2 = jnp.split(t, 2, axis=-1)
    cos = jnp.cos(ang)[:, None, :]
    sin = jnp.sin(ang)[:, None, :]
    return jnp.concatenate([t1 * cos - t2 * sin, t2 * cos + t1 * sin], axis=-1)


def _axial_rope(t, angles):
    ang_row, ang_col = angles
    t_nope, t_row, t_col = jnp.split(t, (QK_NOPE, QK_NOPE + ROPE_AXIS), axis=-1)
    return jnp.concatenate([t_nope, _rotate_half(t_row, ang_row), _rotate_half(t_col, ang_col)], axis=-1)


def _dft_time_matrix(n):
    idx = jnp.arange(n, dtype=jnp.int32)
    kt = (idx[:, None] * idx[None, :]) % n
    ang = kt.astype(F32) * (2.0 * math.pi / n)
    return jnp.concatenate([jnp.cos(ang), -jnp.sin(ang)], axis=1).astype(BF16)


def _dft_channel_matrix():
    idx = jnp.arange(FNO_GROUP_DIM, dtype=jnp.int32)
    ang = ((idx[:, None] * idx[None, :]) % FNO_GROUP_DIM).astype(F32) * (2.0 * math.pi / FNO_GROUP_DIM)
    eye = jnp.eye(FNO_GROUPS, dtype=F32)
    cb = jnp.kron(eye, jnp.cos(ang))
    sb = jnp.kron(eye, jnp.sin(ang))
    return jnp.concatenate([cb, sb], axis=1).astype(BF16)


def _to_heads(t):
    B, T, _ = t.shape
    return t.reshape(B, T, RWKV_HEADS, RWKV_HEAD_DIM).transpose(0, 2, 1, 3).reshape(
        B * RWKV_HEADS, T, RWKV_HEAD_DIM)


def _from_heads(t, B):
    _, T, N = t.shape
    return t.reshape(B, RWKV_HEADS, T, N).transpose(0, 2, 1, 3)


def _layer(x, cs, c_act, cctx_act, p, last, consts):
    B, Tx, D = x.shape
    Tc = cs.shape[1]
    T = Tc + Tx
    dft_x, dft_c, dft_ch, angles = consts

    act = jnp.concatenate([c_act, cctx_act[None, :], jnp.zeros((3, D), F32)], axis=0)
    mod = matmul_f32_bias(act, p["w_mod"], p["b_mod"])
    mod_x = [m[:, None, :] for m in jnp.split(mod[:B], 6, axis=-1)]
    mod_c = jnp.split(mod[B], 6, axis=-1)

    hc = _rms_norm(cs, p["g_norm1"]) * (1.0 + mod_c[1]) + mod_c[0]
    hx = _rms_norm(x, p["g_norm1"]) * (1.0 + mod_x[1]) + mod_x[0]
    h = jnp.concatenate([hc, hx], axis=1).reshape(B * T, D).astype(BF16)

    w_in = p["w_in"]
    o1 = N_RWKV_IN
    o2 = o1 + N_MLA_IN
    o3 = o2 + FNO_WIDTH
    zr = matmul(h, w_in[:, :o1], tn=640).reshape(B, T, N_RWKV_IN)
    w_m = jnp.pad(w_in[:, o1:o2], ((0, 0), (0, 640 - N_MLA_IN)))
    zm = matmul(h, w_m).reshape(B, T, 640)
    zf = matmul(h, w_in[:, o2:o3], out_dtype=BF16)
    zg = matmul(h, w_in[:, o3:]).reshape(B, T, 3, D)

    zr = jnp.concatenate([_shift_mix(zr[:, :Tc], p["mu_prev"], p["mu_next"]),
                          _shift_mix(zr[:, Tc:], p["mu_prev"], p["mu_next"])], axis=1)
    W = RWKV_WIDTH
    r, k, v = zr[..., :W], zr[..., W:2 * W], zr[..., 2 * W:3 * W]
    zw = jnp.tanh(zr[..., 3 * W:3 * W + 128]).reshape(B * T, N_DIR, DECAY_LORA)
    za = zr[..., 3 * W + 128:3 * W + 256].reshape(B * T, N_DIR, AAA_LORA)
    zgate = zr[..., 3 * W + 256:].reshape(B * T, GATE_LORA)
    lw_d, a_d = [], []
    for d in range(N_DIR):
        w_pre = p["w0"][d] + matmul(zw[:, d], p["w2"][d])
        lw_d.append((-math.exp(-0.5)) * jax.nn.sigmoid(w_pre))
        a_d.append(jax.nn.sigmoid(p["a0"][d] + matmul(za[:, d], p["a2"][d])))
    g_out = matmul(jax.nn.sigmoid(zgate), p["g2"]).reshape(B, T, W)
    kkf = (k * p["k_k"]).reshape(B, T, RWKV_HEADS, RWKV_HEAD_DIM)
    kkf = kkf / jnp.maximum(jnp.sqrt(jnp.sum(kkf * kkf, axis=-1, keepdims=True)), 1e-12)
    kkf = kkf.reshape(B, T, W)
    k_dir = [k * (1.0 + (a_d[d].reshape(B, T, W) - 1.0) * p["k_a"]) for d in range(N_DIR)]
    b_dir = [kkf * a_d[d].reshape(B, T, W) for d in range(N_DIR)]
    lw_d = [t.reshape(B, T, W) for t in lw_d]

    def bwd_order(t):
        return jnp.concatenate([jnp.flip(t[:, :Tc], axis=1), jnp.flip(t[:, Tc:], axis=1)], axis=1)

    def stack(tf, tb):
        return jnp.concatenate([_to_heads(tf), _to_heads(bwd_order(tb))], axis=0)

    y = rwkv_scan(stack(r, r), stack(lw_d[0], lw_d[1]), stack(k_dir[0], k_dir[1]),
                  stack(v, v), stack(kkf, kkf), stack(b_dir[0], b_dir[1]))
    nh = B * RWKV_HEADS
    y = _from_heads(y[:nh], B) + bwd_order(_from_heads(y[nh:], B))

    def rwkv_out(y, sl):
        rr = r[:, sl].reshape(B, -1, RWKV_HEADS, RWKV_HEAD_DIM)
        vv = v[:, sl].reshape(B, -1, RWKV_HEADS, RWKV_HEAD_DIM)
        ks = (k_dir[0] + k_dir[1])[:, sl].reshape(B, -1, RWKV_HEADS, RWKV_HEAD_DIM)
        y = y[:, sl]
        n = y.shape[1]
        mu = jnp.mean(y, axis=-1, keepdims=True)
        var = jnp.mean(jnp.square(y - mu), axis=-1, keepdims=True)
        yn = ((y - mu) * lax.rsqrt(var + GN_EPS)).reshape(B, n, W) * p["gn_w"] + p["gn_b"]
        bonus = jnp.sum(rr * p["r_k"] * ks, axis=-1, keepdims=True) * vv
        o = (yn + bonus.reshape(B, n, W)) * g_out[:, sl]
        return matmul(o.reshape(B * n, W), p["w_rwkv_o"]).reshape(B, n, D)

    c_q, c_kv, k_rope = zm[..., :Q_LORA], zm[..., Q_LORA:Q_LORA + KV_LORA], zm[..., Q_LORA + KV_LORA:N_MLA_IN]
    q = matmul(_rms_norm(c_q, p["g_cq"]).reshape(B * T, Q_LORA), p["w_uq"]).reshape(B, T, MLA_HEADS, QK_HEAD)
    kv = matmul(_rms_norm(c_kv, p["g_ckv"]).reshape(B * T, KV_LORA), p["w_ukv"]).reshape(
        B, T, MLA_HEADS, QK_NOPE + V_HEAD)
    k_nope, v_att = kv[..., :QK_NOPE], kv[..., QK_NOPE:]
    k_att = jnp.concatenate(
        [k_nope, jnp.broadcast_to(k_rope[:, :, None, :], (B, T, MLA_HEADS, QK_ROPE))], axis=-1)
    q = _rms_norm(q, p["g_qn"])
    k_att = _rms_norm(k_att, p["g_kn"])
    q = jnp.concatenate([q[:, :Tc], _axial_rope(q[:, Tc:], angles)], axis=1) * (QK_HEAD ** -0.5)
    k_att = jnp.concatenate([k_att[:, :Tc], _axial_rope(k_att[:, Tc:], angles)], axis=1)
    pad = ((0, 0), (0, 0), (0, 0), (0, QK_PAD - QK_HEAD))
    qh = jnp.pad(q, pad).transpose(0, 2, 1, 3).astype(BF16)
    kh = jnp.pad(k_att, pad).transpose(0, 2, 1, 3).astype(BF16)
    vh = v_att.transpose(0, 2, 1, 3).astype(BF16)

    def attn_out(o):
        n = o.shape[2]
        o = o.transpose(0, 2, 1, 3).reshape(B * n, MLA_HEADS * V_HEAD)
        return matmul(o, p["w_mla_o"]).reshape(B, n, D)

    zdft = matmul(zf, dft_ch, out_dtype=BF16).reshape(B, T, 2 * FNO_WIDTH)

    def fourier_out(zz, w, n):
        yf = time_dft(w, zz, 1.0 / math.sqrt(n * FNO_GROUP_DIM))
        return matmul(yf.reshape(B * n, FNO_WIDTH), p["w_fno"]).reshape(B, n, D)

    def finish(res, sl, a_o, b_o, f_o, md):
        g = jax.nn.sigmoid(zg[:, sl])
        n = res.shape[1]
        merged = g[:, :, 0] * a_o + g[:, :, 1] * b_o + g[:, :, 2] * f_o
        mid = res + md[2] * matmul(merged.reshape(B * n, D), p["w_out"]).reshape(B, n, D)
        h2 = _rms_norm(mid, p["g_norm2"]) * (1.0 + md[4]) + md[3]
        hh = matmul(h2.reshape(B * n, D), p["w_ffn_in"])
        hid = hh.shape[1] // 2
        act2 = jax.nn.silu(hh[:, :hid]) * hh[:, hid:]
        return mid + md[5] * matmul(act2, p["w_ffn_out"]).reshape(B, n, D)

    sx = slice(Tc, T)
    a_x = rwkv_out(y, sx)
    b_x = attn_out(attention(qh[:, :, Tc:], kh, vh))
    f_x = fourier_out(zdft[:, Tc:], dft_x, Tx)
    x_new = finish(x, sx, a_x, b_x, f_x, mod_x)
    if last:
        return x_new, cs
    sc = slice(0, Tc)
    a_c = rwkv_out(y, sc)
    b_c = attn_out(attention(qh[:, :, :Tc], kh[:, :, :Tc], vh[:, :, :Tc]))
    f_c = fourier_out(zdft[:, :Tc], dft_c, Tc)
    cs_new = finish(cs, sc, a_c, b_c, f_c, mod_c)
    return x_new, cs_new


def kernel(x, c, ctx, c_ctx, w_mod, b_mod, g_norm1, w_in, mu_prev, mu_next, w0, w2, a0, a2, k_k, k_a, r_k, g2, gn_w, gn_b, w_rwkv_o, g_cq, g_ckv, w_uq, w_ukv, g_qn, g_kn, w_mla_o, w_fno, w_out, g_norm2, w_ffn_in, w_ffn_out):
    layer_params = dict(w_mod=w_mod, b_mod=b_mod, g_norm1=g_norm1, w_in=w_in, mu_prev=mu_prev,
                        mu_next=mu_next, w0=w0, w2=w2, a0=a0, a2=a2, k_k=k_k, k_a=k_a, r_k=r_k,
                        g2=g2, gn_w=gn_w, gn_b=gn_b, w_rwkv_o=w_rwkv_o, g_cq=g_cq, g_ckv=g_ckv,
                        w_uq=w_uq, w_ukv=w_ukv, g_qn=g_qn, g_kn=g_kn, w_mla_o=w_mla_o, w_fno=w_fno,
                        w_out=w_out, g_norm2=g_norm2, w_ffn_in=w_ffn_in, w_ffn_out=w_ffn_out)
    depth = w_mod.shape[0]
    S = x.shape[1]
    consts = (_dft_time_matrix(S), _dft_time_matrix(ctx.shape[1]), _dft_channel_matrix(),
              _rope_tables(S))
    c_act = jax.nn.silu(c)
    cctx_act = jax.nn.silu(c_ctx)
    cs = ctx
    for layer in range(depth):
        p = {name: arr[layer] for name, arr in layer_params.items()}
        x, cs = _layer(x, cs, c_act, cctx_act, p, layer == depth - 1, consts)
    return x
```

```python
import functools
import math

import jax
import jax.numpy as jnp
from jax import lax
from jax.experimental import pallas as pl
from jax.experimental.pallas import tpu as pltpu

F32 = jnp.float32
BF16 = jnp.bfloat16

D_MODEL = 1024
GRID_W = 64
RWKV_HEADS = 8
RWKV_HEAD_DIM = 64
RWKV_WIDTH = RWKV_HEADS * RWKV_HEAD_DIM
DECAY_LORA = 64
AAA_LORA = 64
GATE_LORA = 128
N_DIR = 2
GN_EPS = 64e-5
MLA_HEADS = 8
QK_NOPE = 64
QK_ROPE = 32
QK_HEAD = QK_NOPE + QK_ROPE
V_HEAD = 64
Q_LORA = 384
KV_LORA = 128
ROPE_AXIS = QK_ROPE // 2
ROPE_THETA = 10000.0
FNO_GROUPS = 8
FNO_GROUP_DIM = 64
FNO_WIDTH = FNO_GROUPS * FNO_GROUP_DIM
N_RWKV_IN = 3 * RWKV_WIDTH + N_DIR * DECAY_LORA + N_DIR * AAA_LORA + GATE_LORA
N_MLA_IN = Q_LORA + KV_LORA + QK_ROPE
RMS_EPS = 1e-6

LANES = 128
CHUNK = 64
QK_PAD = LANES
VMEM_LIMIT = 48 * 1024 * 1024


def _mm_body(a_ref, b_ref, o_ref, *, scale):
    acc = jnp.dot(a_ref[...], b_ref[...], preferred_element_type=F32)
    if scale != 1.0:
        acc = acc * scale
    o_ref[...] = acc.astype(o_ref.dtype)


def _mm_acc_body(a_ref, b_ref, o_ref, acc_ref, *, nk, scale):
    k = pl.program_id(2)

    @pl.when(k == 0)
    def _():
        acc_ref[...] = jnp.zeros_like(acc_ref)

    acc_ref[...] += jnp.dot(a_ref[...], b_ref[...], preferred_element_type=F32)

    @pl.when(k == nk - 1)
    def _():
        acc = acc_ref[...]
        if scale != 1.0:
            acc = acc * scale
        o_ref[...] = acc.astype(o_ref.dtype)


def _pick(dim, pref):
    t = min(dim, pref)
    while dim % t:
        t //= 2
    return t


def matmul(a, b, out_dtype=F32, tm=1024, tn=1024, tk=4096, scale=1.0):
    a = a.astype(BF16)
    b = b.astype(BF16)
    M, K = a.shape
    _, N = b.shape
    tm, tn, tk = _pick(M, tm), _pick(N, tn), _pick(K, tk)
    nk = K // tk
    params = pltpu.CompilerParams(
        dimension_semantics=("parallel", "parallel") + (("arbitrary",) if nk > 1 else ()),
        vmem_limit_bytes=VMEM_LIMIT)
    if nk == 1:
        return pl.pallas_call(
            functools.partial(_mm_body, scale=scale),
            grid=(M // tm, N // tn),
            in_specs=[pl.BlockSpec((tm, K), lambda i, j: (i, 0)),
                      pl.BlockSpec((K, tn), lambda i, j: (0, j))],
            out_specs=pl.BlockSpec((tm, tn), lambda i, j: (i, j)),
            out_shape=jax.ShapeDtypeStruct((M, N), out_dtype),
            compiler_params=params,
            name="mm",
        )(a, b)
    return pl.pallas_call(
        functools.partial(_mm_acc_body, nk=nk, scale=scale),
        grid=(M // tm, N // tn, nk),
        in_specs=[pl.BlockSpec((tm, tk), lambda i, j, k: (i, k)),
                  pl.BlockSpec((tk, tn), lambda i, j, k: (k, j))],
        out_specs=pl.BlockSpec((tm, tn), lambda i, j, k: (i, j)),
        out_shape=jax.ShapeDtypeStruct((M, N), out_dtype),
        scratch_shapes=[pltpu.VMEM((tm, tn), F32)],
        compiler_params=params,
        name="mm_acc",
    )(a, b)


def _mm_f32_body(a_ref, b_ref, bias_ref, o_ref):
    o_ref[...] = jnp.dot(a_ref[...], b_ref[...], preferred_element_type=F32,
                         precision=lax.Precision.HIGHEST) + bias_ref[...]


def matmul_f32_bias(a, b, bias, tn=512):
    M, K = a.shape
    _, N = b.shape
    return pl.pallas_call(
        _mm_f32_body,
        grid=(N // tn,),
        in_specs=[pl.BlockSpec((M, K), lambda j: (0, 0)),
                  pl.BlockSpec((K, tn), lambda j: (0, j)),
                  pl.BlockSpec((1, tn), lambda j: (0, j))],
        out_specs=pl.BlockSpec((M, tn), lambda j: (0, j)),
        out_shape=jax.ShapeDtypeStruct((M, N), F32),
        compiler_params=pltpu.CompilerParams(dimension_semantics=("parallel",),
                                             vmem_limit_bytes=VMEM_LIMIT),
        name="mm_f32_bias",
    )(a, b, bias.reshape(1, N))


def _dft_body(w_ref, z_ref, o_ref, acc_ref, *, nk, scale):
    k = pl.program_id(2)

    @pl.when(k == 0)
    def _():
        acc_ref[...] = jnp.zeros_like(acc_ref)

    acc_ref[...] += jnp.dot(w_ref[...], z_ref[...], preferred_element_type=F32)

    @pl.when(k == nk - 1)
    def _():
        o_ref[...] = (acc_ref[...] * scale).astype(o_ref.dtype)


def time_dft(w, z, scale, row0, tm=1024, tk=1024):
    n = w.shape[0]
    B, _, W2 = z.shape
    Wd = W2 // 2
    tm, tk = _pick(n, tm), _pick(n, tk)
    nkh = n // tk
    nk = 2 * nkh
    assert row0 % tk == 0
    off = row0 // tk
    return pl.pallas_call(
        functools.partial(_dft_body, nk=nk, scale=scale),
        grid=(B, n // tm, nk),
        in_specs=[pl.BlockSpec((tm, tk), lambda b, i, k: (i, k)),
                  pl.BlockSpec((None, tk, Wd), lambda b, i, k: (b, off + k % nkh, k // nkh))],
        out_specs=pl.BlockSpec((None, tm, Wd), lambda b, i, k: (b, i, 0)),
        out_shape=jax.ShapeDtypeStruct((B, n, Wd), BF16),
        scratch_shapes=[pltpu.VMEM((tm, Wd), F32)],
        compiler_params=pltpu.CompilerParams(
            dimension_semantics=("parallel", "parallel", "arbitrary"),
            vmem_limit_bytes=VMEM_LIMIT),
        name="time_dft",
    )(w, z)


def _dot(a, b):
    return jnp.dot(a, b, preferred_element_type=F32)


def _dot_nt(a, b):
    return lax.dot_general(a, b, (((1,), (1,)), ((), ())), preferred_element_type=F32)


def _dot_tn(a, b):
    return lax.dot_general(a, b, (((0,), (0,)), ((), ())), preferred_element_type=F32)


def _delta_chains(chains, m_a, bdmask):
    C = CHUNK
    bf = lambda t: t.astype(BF16)
    zero = jnp.zeros((), BF16)
    cat = lambda ts: jnp.concatenate(ts, axis=0)

    def blk(t):
        t = bf(t)
        return cat([jnp.where(m_a, t, zero), jnp.where(m_a, zero, t)])

    a4 = [_dot_nt(cat([bf(c["ah"]), bf(c["rh"])]), cat([blk(c["bh"]), blk(c["kh"])])) for c in chains]
    a_ab = [jnp.where(c["strict2"], a[:C, :2 * C], 0.0) for c, a in zip(chains, a4)]
    a_ak = [jnp.where(c["strict2"], a[:C, 2 * C:], 0.0) for c, a in zip(chains, a4)]
    a_r = [jnp.where(c["incl4"], a[C:], 0.0) for c, a in zip(chains, a4)]
    x = a_ab
    pw = a_ab
    pwb = [blk(t) for t in pw]
    for _ in range(int(math.log2(C)) - 1):
        pw = [_dot(bf(t), tb) for t, tb in zip(pw, pwb)]
        pwb = [blk(t) for t in pw]
        x = [xi + t + _dot(bf(xi), tb) for xi, t, tb in zip(x, pw, pwb)]
    spb = [bf(c["sp"]) for c in chains]
    vb = [blk(c["v"]) for c in chains]
    x1 = [_dot_nt(bf(c["ah"]), s) + _dot(bf(a), v) for c, s, a, v in zip(chains, spb, a_ak, vb)]
    u = [t + _dot(bf(xi), blk(t)) for t, xi in zip(x1, x)]
    y = [_dot_nt(bf(c["rh"]), s) + _dot(bf(a), cat([blk(ui), v]))
         for c, s, a, ui, v in zip(chains, spb, a_r, u, vb)]
    upd = [_dot_tn(cat([bf(ui), bf(c["v"])]), cat([bf(c["bb"]), bf(c["kb"])])) for c, ui in zip(chains, u)]
    sp_new = [c["sp"] * c["dtot"] + jnp.where(bdmask, t, 0.0) for c, t in zip(chains, upd)]
    return list(zip(y, sp_new))


def _split3(t):
    hi = t.astype(BF16)
    r1 = t - hi.astype(F32)
    mid = r1.astype(BF16)
    lo = (r1 - mid.astype(F32)).astype(BF16)
    return hi, mid, lo


def _delta_prep(r, lw, k, v, kk, b, rev):
    C = CHUNK

    def causal(width, strict):
        ti = lax.broadcasted_iota(jnp.int32, (C, width), 0)
        ii = lax.broadcasted_iota(jnp.int32, (C, width), 1) & (C - 1)
        if rev:
            return ii > ti if strict else ii >= ti
        return ii < ti if strict else ii <= ti

    tri = causal(C, False).astype(BF16)
    L = sum(_dot(tri, part) for part in _split3(lw))
    last = 0 if rev else C - 1
    ltot = L[last:last + 1, :]
    e_nl = jnp.exp(-L)
    e_t = jnp.exp(ltot - L)
    return dict(ah=-kk * jnp.exp(L - lw), rh=r * jnp.exp(L), bh=b * e_nl, kh=k * e_nl, bb=b * e_t,
                kb=k * e_t, v=v, dtot=jnp.exp(ltot)), causal(2 * C, True), causal(4 * C, False)


def _delta_body(rf_ref, vf_ref, kkf_ref, lwf_ref, kf_ref, bf_ref,
                rb_ref, vb_ref, kkb_ref, lwb_ref, kb_ref, bb_ref,
                yf_ref, yb_ref, s_ref):
    @pl.when(pl.program_id(1) == 0)
    def _():
        s_ref[...] = jnp.zeros_like(s_ref)

    nb, C, W = rf_ref.shape
    m_a = lax.broadcasted_iota(jnp.int32, (C, LANES), 1) < (LANES // 2)
    br = lax.broadcasted_iota(jnp.int32, (LANES, LANES), 0) < (LANES // 2)
    bc = lax.broadcasted_iota(jnp.int32, (LANES, LANES), 1) < (LANES // 2)
    bdmask = br == bc
    dirs = ((rf_ref, lwf_ref, kf_ref, vf_ref, kkf_ref, bf_ref, yf_ref, False),
            (rb_ref, lwb_ref, kb_ref, vb_ref, kkb_ref, bb_ref, yb_ref, True))
    chains, dest = [], []
    for d, (r_ref, lw_ref, k_ref, v_ref, kk_ref, b_ref, y_ref, rev) in enumerate(dirs):
        for j in range(nb):
            ops, strict2, incl4 = _delta_prep(r_ref[j], lw_ref[j], k_ref[j], v_ref[j], kk_ref[j],
                                              b_ref[j], rev)
            for p in range(W // LANES):
                sl = slice(p * LANES, (p + 1) * LANES)
                c = {name: t[:, sl] for name, t in ops.items()}
                c.update(sp=s_ref[d, j, p], strict2=strict2, incl4=incl4)
                chains.append(c)
                dest.append((y_ref, d, j, p, sl))
    for (y, sp_new), (y_ref, d, j, p, sl) in zip(_delta_chains(chains, m_a, bdmask), dest):
        y_ref[j, :, sl] = y
        s_ref[d, j, p] = sp_new


def rwkv_scan(r, v, kk, lw, k, b, n_ctx, samples=2):
    B, T, W = r.shape
    nc = T // CHUNK
    ncc = n_ctx // CHUNK

    def fwd_chunk(c):
        return jnp.where(c < ncc, nc - ncc + c, c - ncc)

    def bwd_chunk(c):
        return nc - 1 - c

    nb = _pick(B, samples)
    shared_f = pl.BlockSpec((nb, CHUNK, W), lambda bi, c: (bi, fwd_chunk(c), 0))
    shared_b = pl.BlockSpec((nb, CHUNK, W), lambda bi, c: (bi, bwd_chunk(c), 0))
    dir_f = pl.BlockSpec((None, nb, CHUNK, W), lambda bi, c: (0, bi, fwd_chunk(c), 0))
    dir_b = pl.BlockSpec((None, nb, CHUNK, W), lambda bi, c: (1, bi, bwd_chunk(c), 0))
    return pl.pallas_call(
        _delta_body,
        grid=(B // nb, nc),
        in_specs=[shared_f] * 3 + [dir_f] * 3 + [shared_b] * 3 + [dir_b] * 3,
        out_specs=[shared_f, shared_b],
        out_shape=[jax.ShapeDtypeStruct((B, T, W), F32)] * 2,
        scratch_shapes=[pltpu.VMEM((N_DIR, nb, W // LANES, LANES, LANES), F32)],
        compiler_params=pltpu.CompilerParams(
            dimension_semantics=("parallel", "arbitrary"),
            vmem_limit_bytes=VMEM_LIMIT),
        name="delta_scan",
    )(r, v, kk, lw, k, b, r, v, kk, lw, k, b)


def _attn_body(q_ref, k_ref, v_ref, o_ref):
    outs = []
    for hh in range(2):
        sl = slice(hh * QK_PAD, (hh + 1) * QK_PAD)
        s = _dot_nt(q_ref[:, sl], k_ref[:, sl])
        m = jnp.max(s, axis=-1, keepdims=True)
        p = jnp.exp(s - m)
        l = jnp.sum(p, axis=-1, keepdims=True)
        outs.append(_dot(p.astype(BF16), v_ref[...]) / l)
    lane = lax.broadcasted_iota(jnp.int32, outs[0].shape, 1)
    o_ref[...] = jnp.where(lane < V_HEAD, outs[0], outs[1])


def attention(q, k, v, q_row0, n_q, k_row0, n_k, tq=256):
    B, _, HD = q.shape
    H = HD // QK_PAD
    tq = _pick(n_q, tq)
    assert q_row0 % tq == 0 and k_row0 % n_k == 0
    qoff = q_row0 // tq
    koff = k_row0 // n_k
    return pl.pallas_call(
        _attn_body,
        grid=(B, H // 2, n_q // tq),
        in_specs=[pl.BlockSpec((None, tq, 2 * QK_PAD), lambda b, h, i: (b, qoff + i, h)),
                  pl.BlockSpec((None, n_k, 2 * QK_PAD), lambda b, h, i: (b, koff, h)),
                  pl.BlockSpec((None, n_k, 2 * V_HEAD), lambda b, h, i: (b, koff, h))],
        out_specs=pl.BlockSpec((None, tq, 2 * V_HEAD), lambda b, h, i: (b, i, h)),
        out_shape=jax.ShapeDtypeStruct((B, n_q, H * V_HEAD), F32),
        compiler_params=pltpu.CompilerParams(
            dimension_semantics=("parallel", "parallel", "parallel"),
            vmem_limit_bytes=VMEM_LIMIT),
        name="attention",
    )(q, k, v)


def _rms_norm(x, g):
    y = x * lax.rsqrt(jnp.mean(x * x, axis=-1, keepdims=True) + RMS_EPS)
    return y * g


def _shift_mix(z, mu_prev, mu_next):
    z_prev = jnp.pad(z, ((0, 0), (1, 0), (0, 0)))[:, :-1]
    z_next = jnp.pad(z, ((0, 0), (0, 1), (0, 0)))[:, 1:]
    return z + (z_prev - z) * mu_prev + (z_next - z) * mu_next


def _rope_tables(n_tokens):
    rows = n_tokens // GRID_W
    row = jnp.broadcast_to(jnp.arange(rows)[:, None], (rows, GRID_W)).reshape(-1).astype(F32)
    col = jnp.broadcast_to(jnp.arange(GRID_W)[None, :], (rows, GRID_W)).reshape(-1).astype(F32)
    inv_freq = ROPE_THETA ** (-jnp.arange(0, ROPE_AXIS, 2, dtype=F32) / ROPE_AXIS)
    return row[:, None] * inv_freq, col[:, None] * inv_freq


def _rotate_half(t, ang):
    t1, t2 = jnp.split(t, 2, axis=-1)
    cos = jnp.cos(ang)[:, None, :]
    sin = jnp.sin(ang)[:, None, :]
    return jnp.concatenate([t1 * cos - t2 * sin, t2 * cos + t1 * sin], axis=-1)


def _qk_finish(t, gain, angles, n_lat, scale):
    B, T, H, _ = t.shape
    ms = jnp.sum(t * t, axis=-1, keepdims=True) * (1.0 / QK_HEAD)
    t = t * lax.rsqrt(ms + RMS_EPS)
    t = t[..., :QK_HEAD] * gain
    tx = t[:, :n_lat]
    ang_row, ang_col = angles
    t_nope, t_row, t_col = jnp.split(tx, (QK_NOPE, QK_NOPE + ROPE_AXIS), axis=-1)
    tx = jnp.concatenate([t_nope, _rotate_half(t_row, ang_row), _rotate_half(t_col, ang_col)], axis=-1)
    t = jnp.concatenate([tx, t[:, n_lat:]], axis=1) * scale
    t = jnp.pad(t, ((0, 0), (0, 0), (0, 0), (0, QK_PAD - QK_HEAD)))
    return t.reshape(B, T, H * QK_PAD).astype(BF16)


def _dft_time_matrix(n):
    idx = jnp.arange(n, dtype=jnp.int32)
    kt = (idx[:, None] * idx[None, :]) % n
    ang = kt.astype(F32) * (2.0 * math.pi / n)
    return jnp.concatenate([jnp.cos(ang), -jnp.sin(ang)], axis=1).astype(BF16)


def _dft_channel_matrix():
    idx = jnp.arange(FNO_GROUP_DIM, dtype=jnp.int32)
    ang = ((idx[:, None] * idx[None, :]) % FNO_GROUP_DIM).astype(F32) * (2.0 * math.pi / FNO_GROUP_DIM)
    eye = jnp.eye(FNO_GROUPS, dtype=F32)
    cb = jnp.kron(eye, jnp.cos(ang))
    sb = jnp.kron(eye, jnp.sin(ang))
    return jnp.concatenate([cb, sb], axis=1).astype(BF16)


def _layer(x, cs, c_act, cctx_act, p, last, consts):
    B, Tx, D = x.shape
    Tc = cs.shape[1]
    T = Tc + Tx
    dft_x, dft_c, dft_ch, angles = consts

    act = jnp.concatenate([c_act, cctx_act[None, :], jnp.zeros((3, D), F32)], axis=0)
    mod = matmul_f32_bias(act, p["w_mod"], p["b_mod"])
    mod_x = [m[:, None, :] for m in jnp.split(mod[:B], 6, axis=-1)]
    mod_c = jnp.split(mod[B], 6, axis=-1)

    hc = _rms_norm(cs, p["g_norm1"]) * (1.0 + mod_c[1]) + mod_c[0]
    hx = _rms_norm(x, p["g_norm1"]) * (1.0 + mod_x[1]) + mod_x[0]
    h = jnp.concatenate([hx, hc], axis=1).reshape(B * T, D).astype(BF16)

    w_in = p["w_in"]
    o1 = N_RWKV_IN
    o2 = o1 + N_MLA_IN
    o3 = o2 + FNO_WIDTH
    zr = matmul(h, w_in[:, :o1], tn=640).reshape(B, T, N_RWKV_IN)
    w_m = jnp.pad(w_in[:, o1:o2], ((0, 0), (0, 640 - N_MLA_IN)))
    zm = matmul(h, w_m).reshape(B, T, 640)
    zf = matmul(h, w_in[:, o2:o3], out_dtype=BF16)
    zg = matmul(h, w_in[:, o3:]).reshape(B, T, 3, D)

    zr = jnp.concatenate([_shift_mix(zr[:, :Tx], p["mu_prev"], p["mu_next"]),
                          _shift_mix(zr[:, Tx:], p["mu_prev"], p["mu_next"])], axis=1)
    W = RWKV_WIDTH
    r, k, v = zr[..., :W], zr[..., W:2 * W], zr[..., 2 * W:3 * W]
    zw = jnp.tanh(zr[..., 3 * W:3 * W + 128]).reshape(B * T, N_DIR, DECAY_LORA)
    za = zr[..., 3 * W + 128:3 * W + 256].reshape(B * T, N_DIR, AAA_LORA)
    zgate = zr[..., 3 * W + 256:].reshape(B * T, GATE_LORA)
    lw_d, a_d = [], []
    for d in range(N_DIR):
        w_pre = p["w0"][d] + matmul(zw[:, d], p["w2"][d])
        lw_d.append(((-math.exp(-0.5)) * jax.nn.sigmoid(w_pre)).reshape(B, T, W))
        a_d.append(jax.nn.sigmoid(p["a0"][d] + matmul(za[:, d], p["a2"][d])).reshape(B, T, W))
    g_out = matmul(jax.nn.sigmoid(zgate), p["g2"]).reshape(B, T, W)
    kkf = (k * p["k_k"]).reshape(B, T, RWKV_HEADS, RWKV_HEAD_DIM)
    kkf = kkf / jnp.maximum(jnp.sqrt(jnp.sum(kkf * kkf, axis=-1, keepdims=True)), 1e-12)
    kkf = kkf.reshape(B, T, W)
    k_dir = jnp.stack([k * (1.0 + (a_d[d] - 1.0) * p["k_a"]) for d in range(N_DIR)])
    b_dir = jnp.stack([kkf * a_d[d] for d in range(N_DIR)])
    y_f, y_b = rwkv_scan(r, v, kkf, jnp.stack(lw_d), k_dir, b_dir, Tc)
    y = (y_f + y_b).reshape(B, T, RWKV_HEADS, RWKV_HEAD_DIM)
    k_sum = k_dir[0] + k_dir[1]

    def rwkv_out(sl):
        hd = lambda t: t[:, sl].reshape(B, -1, RWKV_HEADS, RWKV_HEAD_DIM)
        ys = y[:, sl]
        n = ys.shape[1]
        mu = jnp.mean(ys, axis=-1, keepdims=True)
        var = jnp.mean(jnp.square(ys - mu), axis=-1, keepdims=True)
        yn = ((ys - mu) * lax.rsqrt(var + GN_EPS)).reshape(B, n, W) * p["gn_w"] + p["gn_b"]
        bonus = jnp.sum(hd(r) * p["r_k"] * hd(k_sum), axis=-1, keepdims=True) * hd(v)
        o = (yn + bonus.reshape(B, n, W)) * g_out[:, sl]
        return matmul(o.reshape(B * n, W), p["w_rwkv_o"]).reshape(B, n, D)

    c_q, c_kv, k_rope = zm[..., :Q_LORA], zm[..., Q_LORA:Q_LORA + KV_LORA], zm[..., Q_LORA + KV_LORA:N_MLA_IN]
    hpad = ((0, 0), (0, 0), (0, QK_PAD - QK_HEAD))
    w_uq = jnp.pad(p["w_uq"].reshape(Q_LORA, MLA_HEADS, QK_HEAD), hpad).reshape(Q_LORA, MLA_HEADS * QK_PAD)
    w_ukv = p["w_ukv"].reshape(KV_LORA, MLA_HEADS, QK_NOPE + V_HEAD)
    w_uk = w_ukv[..., :QK_NOPE].reshape(KV_LORA, MLA_HEADS * QK_NOPE)
    w_uv = w_ukv[..., QK_NOPE:].reshape(KV_LORA, MLA_HEADS * V_HEAD)
    cq_n = _rms_norm(c_q, p["g_cq"]).reshape(B * T, Q_LORA)
    ckv_n = _rms_norm(c_kv, p["g_ckv"]).reshape(B * T, KV_LORA).astype(BF16)
    q = matmul(cq_n, w_uq).reshape(B, T, MLA_HEADS, QK_PAD)
    k_nope = matmul(ckv_n, w_uk).reshape(B, T, MLA_HEADS, QK_NOPE)
    vh = matmul(ckv_n, w_uv, out_dtype=BF16).reshape(B, T, MLA_HEADS * V_HEAD)
    k_att = jnp.concatenate(
        [k_nope, jnp.broadcast_to(k_rope[:, :, None, :], (B, T, MLA_HEADS, QK_ROPE)),
         jnp.zeros((B, T, MLA_HEADS, QK_PAD - QK_HEAD), F32)], axis=-1)
    qh = _qk_finish(q, p["g_qn"], angles, Tx, QK_HEAD ** -0.5)
    kh = _qk_finish(k_att, p["g_kn"], angles, Tx, 1.0)

    def attn_out(o):
        n = o.shape[1]
        return matmul(o.reshape(B * n, MLA_HEADS * V_HEAD), p["w_mla_o"]).reshape(B, n, D)

    zdft = matmul(zf, dft_ch, out_dtype=BF16).reshape(B, T, 2 * FNO_WIDTH)

    def fourier_out(w, row0, n):
        yf = time_dft(w, zdft, 1.0 / math.sqrt(n * FNO_GROUP_DIM), row0)
        return matmul(yf.reshape(B * n, FNO_WIDTH), p["w_fno"]).reshape(B, n, D)

    def finish(res, sl, a_o, b_o, f_o, md):
        g = jax.nn.sigmoid(zg[:, sl])
        n = res.shape[1]
        merged = g[:, :, 0] * a_o + g[:, :, 1] * b_o + g[:, :, 2] * f_o
        mid = res + md[2] * matmul(merged.reshape(B * n, D), p["w_out"]).reshape(B, n, D)
        h2 = _rms_norm(mid, p["g_norm2"]) * (1.0 + md[4]) + md[3]
        hh = matmul(h2.reshape(B * n, D), p["w_ffn_in"])
        hid = hh.shape[1] // 2
        act2 = jax.nn.silu(hh[:, :hid]) * hh[:, hid:]
        return mid + md[5] * matmul(act2, p["w_ffn_out"]).reshape(B, n, D)

    sx = slice(0, Tx)
    a_x = rwkv_out(sx)
    b_x = attn_out(attention(qh, kh, vh, 0, Tx, 0, T))
    f_x = fourier_out(dft_x, 0, Tx)
    x_new = finish(x, sx, a_x, b_x, f_x, mod_x)
    if last:
        return x_new, cs
    sc = slice(Tx, T)
    a_c = rwkv_out(sc)
    b_c = attn_out(attention(qh, kh, vh, Tx, Tc, Tx, Tc))
    f_c = fourier_out(dft_c, Tx, Tc)
    cs_new = finish(cs, sc, a_c, b_c, f_c, mod_c)
    return x_new, cs_new


def kernel(x, c, ctx, c_ctx, w_mod, b_mod, g_norm1, w_in, mu_prev, mu_next, w0, w2, a0, a2, k_k, k_a, r_k, g2, gn_w, gn_b, w_rwkv_o, g_cq, g_ckv, w_uq, w_ukv, g_qn, g_kn, w_mla_o, w_fno, w_out, g_norm2, w_ffn_in, w_ffn_out):
    layer_params = dict(w_mod=w_mod, b_mod=b_mod, g_norm1=g_norm1, w_in=w_in, mu_prev=mu_prev,
                        mu_next=mu_next, w0=w0, w2=w2, a0=a0, a2=a2, k_k=k_k, k_a=k_a, r_k=r_k,
                        g2=g2, gn_w=gn_w, gn_b=gn_b, w_rwkv_o=w_rwkv_o, g_cq=g_cq, g_ckv=g_ckv,
                        w_uq=w_uq, w_ukv=w_ukv, g_qn=g_qn, g_kn=g_kn, w_mla_o=w_mla_o, w_fno=w_fno,
                        w_out=w_out, g_norm2=g_norm2, w_ffn_in=w_ffn_in, w_ffn_out=w_ffn_out)
    depth = w_mod.shape[0]
    S = x.shape[1]
    consts = (_dft_time_matrix(S), _dft_time_matrix(ctx.shape[1]), _dft_channel_matrix(),
              _rope_tables(S))
    c_act = jax.nn.silu(c)
    cctx_act = jax.nn.silu(c_ctx)
    cs = ctx
    for layer in range(depth):
        p = {name: arr[layer] for name, arr in layer_params.items()}
        x, cs = _layer(x, cs, c_act, cctx_act, p, layer == depth - 1, consts)
    return x
```

```python
import functools
import math

import jax
import jax.numpy as jnp
from jax import lax
from jax.experimental import pallas as pl
from jax.experimental.pallas import tpu as pltpu

F32 = jnp.float32
BF16 = jnp.bfloat16

D_MODEL = 1024
GRID_W = 64
RWKV_HEADS = 8
RWKV_HEAD_DIM = 64
RWKV_WIDTH = RWKV_HEADS * RWKV_HEAD_DIM
DECAY_LORA = 64
AAA_LORA = 64
GATE_LORA = 128
N_DIR = 2
GN_EPS = 64e-5
MLA_HEADS = 8
QK_NOPE = 64
QK_ROPE = 32
QK_HEAD = QK_NOPE + QK_ROPE
V_HEAD = 64
Q_LORA = 384
KV_LORA = 128
ROPE_AXIS = QK_ROPE // 2
ROPE_THETA = 10000.0
FNO_GROUPS = 8
FNO_GROUP_DIM = 64
FNO_WIDTH = FNO_GROUPS * FNO_GROUP_DIM
N_RWKV_IN = 3 * RWKV_WIDTH + N_DIR * DECAY_LORA + N_DIR * AAA_LORA + GATE_LORA
N_MLA_IN = Q_LORA + KV_LORA + QK_ROPE
RMS_EPS = 1e-6

LANES = 128
CHUNK = 64
QK_PAD = LANES
ROW_TILE = 1024
ZR_PAD = 2048
ZM_PAD = 1024
COL_TILE = 1024
FFN_CHUNK = 256
VMEM_LIMIT = 48 * 1024 * 1024


def _params(*sem):
    return pltpu.CompilerParams(dimension_semantics=sem, vmem_limit_bytes=VMEM_LIMIT)


def _dot(a, b):
    return jnp.dot(a, b, preferred_element_type=F32)


def _dot_nt(a, b):
    return lax.dot_general(a, b, (((1,), (1,)), ((), ())), preferred_element_type=F32)


def _dot_tn(a, b):
    return lax.dot_general(a, b, (((0,), (0,)), ((), ())), preferred_element_type=F32)


def _sigmoid(x):
    return 1.0 / (1.0 + jnp.exp(-x))


def _pick(dim, pref):
    t = min(dim, pref)
    while dim % t:
        t //= 2
    return t


def _mm_body(a_ref, b_ref, o_ref):
    o_ref[...] = _dot(a_ref[...], b_ref[...]).astype(o_ref.dtype)


def matmul(a, b, out_dtype=F32, tm=1024, tn=1024):
    a = a.astype(BF16)
    b = b.astype(BF16)
    M, K = a.shape
    _, N = b.shape
    tm, tn = _pick(M, tm), _pick(N, tn)
    return pl.pallas_call(
        _mm_body,
        grid=(M // tm, N // tn),
        in_specs=[pl.BlockSpec((tm, K), lambda i, j: (i, 0)),
                  pl.BlockSpec((K, tn), lambda i, j: (0, j))],
        out_specs=pl.BlockSpec((tm, tn), lambda i, j: (i, j)),
        out_shape=jax.ShapeDtypeStruct((M, N), out_dtype),
        compiler_params=_params("parallel", "parallel"),
        name="mm",
    )(a, b)


def _mm_f32_body(a_ref, b_ref, bias_ref, o_ref):
    o_ref[...] = jnp.dot(a_ref[...], b_ref[...], preferred_element_type=F32,
                         precision=lax.Precision.HIGHEST) + bias_ref[...]


def matmul_f32_bias(a, b, bias, tn=512):
    M, K = a.shape
    _, N = b.shape
    return pl.pallas_call(
        _mm_f32_body,
        grid=(N // tn,),
        in_specs=[pl.BlockSpec((M, K), lambda j: (0, 0)),
                  pl.BlockSpec((K, tn), lambda j: (0, j)),
                  pl.BlockSpec((1, tn), lambda j: (0, j))],
        out_specs=pl.BlockSpec((M, tn), lambda j: (0, j)),
        out_shape=jax.ShapeDtypeStruct((M, N), F32),
        compiler_params=_params("parallel"),
        name="mm_f32_bias",
    )(a, b, bias.reshape(1, N))


def _norm_mod(x, gain, shift, scale):
    y = x * lax.rsqrt(jnp.mean(x * x, axis=-1, keepdims=True) + RMS_EPS)
    return (y * gain) * (1.0 + scale) + shift


def _inproj_body(x_ref, mod_ref, g_ref, w_ref, zr_ref, zm_ref, zg_ref, zd_ref, h_ref):
    j = pl.program_id(1)

    @pl.when(j == 0)
    def _():
        h_ref[...] = _norm_mod(x_ref[...], g_ref[...], mod_ref[0:1, :], mod_ref[1:2, :]).astype(BF16)

    acc = _dot(h_ref[...], w_ref[...])

    @pl.when(j < 2)
    def _():
        zr_ref[...] = acc

    @pl.when(j == 2)
    def _():
        zm_ref[...] = acc

    @pl.when((j >= 3) & (j < 6))
    def _():
        zg_ref[...] = _sigmoid(acc).astype(BF16)

    @pl.when(j == 6)
    def _():
        zd_ref[...] = acc.astype(BF16)


def _mod_row(i, tiles_per_sample, n_samples):
    return jnp.minimum(i // tiles_per_sample, n_samples)


def inproj(xg, modtab, gain, w_all, n_lat):
    M, D = xg.shape
    n_samples = modtab.shape[0] - 1
    tps = n_lat // ROW_TILE
    nj = w_all.shape[1] // COL_TILE
    assert nj == 7
    clamp = lambda j, lo, hi: jnp.minimum(jnp.maximum(j, lo), hi)
    return pl.pallas_call(
        _inproj_body,
        grid=(M // ROW_TILE, nj),
        in_specs=[pl.BlockSpec((ROW_TILE, D), lambda i, j: (i, 0)),
                  pl.BlockSpec((None, 6, D), lambda i, j: (_mod_row(i, tps, n_samples), 0, 0)),
                  pl.BlockSpec((1, D), lambda i, j: (0, 0)),
                  pl.BlockSpec((D, COL_TILE), lambda i, j: (0, j))],
        out_specs=[pl.BlockSpec((ROW_TILE, COL_TILE), lambda i, j: (i, clamp(j, 0, 1))),
                   pl.BlockSpec((ROW_TILE, COL_TILE), lambda i, j: (i, 0)),
                   pl.BlockSpec((ROW_TILE, COL_TILE), lambda i, j: (i, clamp(j - 3, 0, 2))),
                   pl.BlockSpec((ROW_TILE, COL_TILE), lambda i, j: (i, 0))],
        out_shape=[jax.ShapeDtypeStruct((M, ZR_PAD), F32),
                   jax.ShapeDtypeStruct((M, ZM_PAD), F32),
                   jax.ShapeDtypeStruct((M, 3 * D), BF16),
                   jax.ShapeDtypeStruct((M, 2 * FNO_WIDTH), BF16)],
        scratch_shapes=[pltpu.VMEM((ROW_TILE, D), BF16)],
        compiler_params=_params("parallel", "arbitrary"),
        name="inproj",
    )(xg, modtab, gain.reshape(1, D), w_all)


def _post_body(o_ref, att_ref, yf_ref, zg_ref, x_ref, mod_ref, wr_ref, wm_ref, wf_ref, wo_ref, out_ref):
    D = x_ref.shape[1]
    a = _dot(o_ref[...], wr_ref[...])
    b = _dot(att_ref[...], wm_ref[...])
    f = _dot(yf_ref[...], wf_ref[...])
    merged = (zg_ref[:, 0:D].astype(F32) * a + zg_ref[:, D:2 * D].astype(F32) * b
              + zg_ref[:, 2 * D:3 * D].astype(F32) * f)
    out_ref[...] = x_ref[...] + mod_ref[2:3, :] * _dot(merged.astype(BF16), wo_ref[...])


def post(o, att, yf, zg, xg, modtab, w_r, w_m, w_f, w_o, n_rows, n_lat, tm=512):
    D = xg.shape[1]
    n_samples = modtab.shape[0] - 1
    tps = n_lat // tm
    row = lambda i: (i, 0)
    whole = lambda i: (0, 0)
    wspec = pl.BlockSpec((o.shape[1], D), whole)
    return pl.pallas_call(
        _post_body,
        grid=(n_rows // tm,),
        in_specs=[pl.BlockSpec((tm, o.shape[1]), row), pl.BlockSpec((tm, att.shape[1]), row),
                  pl.BlockSpec((tm, yf.shape[1]), row), pl.BlockSpec((tm, 3 * D), row),
                  pl.BlockSpec((tm, D), row),
                  pl.BlockSpec((None, 6, D), lambda i: (_mod_row(i, tps, n_samples), 0, 0)),
                  wspec, wspec, wspec, pl.BlockSpec((D, D), whole)],
        out_specs=pl.BlockSpec((tm, D), row),
        out_shape=jax.ShapeDtypeStruct((n_rows, D), F32),
        compiler_params=_params("parallel"),
        name="post",
    )(o, att, yf, zg, xg, modtab, w_r.astype(BF16), w_m.astype(BF16), w_f.astype(BF16), w_o.astype(BF16))


def _ffn_body(x_ref, mod_ref, g_ref, wg_ref, wu_ref, wo_ref, out_ref, h_ref, acc_ref, *, n_chunks):
    c = pl.program_id(1)

    @pl.when(c == 0)
    def _():
        h_ref[...] = _norm_mod(x_ref[...], g_ref[...], mod_ref[3:4, :], mod_ref[4:5, :]).astype(BF16)
        acc_ref[...] = jnp.zeros_like(acc_ref)

    h = h_ref[...]
    gate = _dot(h, wg_ref[...])
    up = _dot(h, wu_ref[...])
    act = gate * _sigmoid(gate) * up
    acc_ref[...] += _dot(act.astype(BF16), wo_ref[...])

    @pl.when(c == n_chunks - 1)
    def _():
        out_ref[...] = x_ref[...] + mod_ref[5:6, :] * acc_ref[...]


def ffn(xg, modtab, gain, w_in, w_out, n_lat):
    M, D = xg.shape
    hidden = w_out.shape[0]
    n_samples = modtab.shape[0] - 1
    tps = n_lat // ROW_TILE
    n_chunks = hidden // FFN_CHUNK
    return pl.pallas_call(
        functools.partial(_ffn_body, n_chunks=n_chunks),
        grid=(M // ROW_TILE, n_chunks),
        in_specs=[pl.BlockSpec((ROW_TILE, D), lambda i, c: (i, 0)),
                  pl.BlockSpec((None, 6, D), lambda i, c: (_mod_row(i, tps, n_samples), 0, 0)),
                  pl.BlockSpec((1, D), lambda i, c: (0, 0)),
                  pl.BlockSpec((D, FFN_CHUNK), lambda i, c: (0, c)),
                  pl.BlockSpec((D, FFN_CHUNK), lambda i, c: (0, n_chunks + c)),
                  pl.BlockSpec((FFN_CHUNK, D), lambda i, c: (c, 0))],
        out_specs=pl.BlockSpec((ROW_TILE, D), lambda i, c: (i, 0)),
        out_shape=jax.ShapeDtypeStruct((M, D), F32),
        scratch_shapes=[pltpu.VMEM((ROW_TILE, D), BF16), pltpu.VMEM((ROW_TILE, D), F32)],
        compiler_params=_params("parallel", "arbitrary"),
        name="ffn",
    )(xg, modtab, gain.reshape(1, D), w_in, w_in, w_out)


def _dft_body(w_ref, z_ref, o_ref, acc_ref, *, nk, scale):
    k = pl.program_id(2)

    @pl.when(k == 0)
    def _():
        acc_ref[...] = jnp.zeros_like(acc_ref)

    acc_ref[...] += _dot(w_ref[...], z_ref[...])

    @pl.when(k == nk - 1)
    def _():
        o_ref[...] = (acc_ref[...] * scale).astype(o_ref.dtype)


def time_dft(w, z, n_samples, row0, scale, tm=1024, tk=1024):
    n = w.shape[0]
    Wd = z.shape[1] // 2
    tm, tk = _pick(n, tm), _pick(n, tk)
    nkh = n // tk
    nk = 2 * nkh
    assert row0 % tk == 0
    off = row0 // tk
    return pl.pallas_call(
        functools.partial(_dft_body, nk=nk, scale=scale),
        grid=(n_samples, n // tm, nk),
        in_specs=[pl.BlockSpec((tm, tk), lambda b, i, k: (i, k)),
                  pl.BlockSpec((tk, Wd), lambda b, i, k: (off + b * nkh + k % nkh, k // nkh))],
        out_specs=pl.BlockSpec((tm, Wd), lambda b, i, k: (b * (n // tm) + i, 0)),
        out_shape=jax.ShapeDtypeStruct((n_samples * n, Wd), BF16),
        scratch_shapes=[pltpu.VMEM((tm, Wd), F32)],
        compiler_params=_params("parallel", "parallel", "arbitrary"),
        name="time_dft",
    )(w, z)


def _delta_chains(chains, m_a, bdmask):
    C = CHUNK
    bf = lambda t: t.astype(BF16)
    zero = jnp.zeros((), BF16)
    cat = lambda ts: jnp.concatenate(ts, axis=0)

    def blk(t):
        t = bf(t)
        return cat([jnp.where(m_a, t, zero), jnp.where(m_a, zero, t)])

    a4 = [_dot_nt(cat([bf(c["ah"]), bf(c["rh"])]), cat([blk(c["bh"]), blk(c["kh"])])) for c in chains]
    a_ab = [jnp.where(c["strict2"], a[:C, :2 * C], 0.0) for c, a in zip(chains, a4)]
    a_ak = [jnp.where(c["strict2"], a[:C, 2 * C:], 0.0) for c, a in zip(chains, a4)]
    a_r = [jnp.where(c["incl4"], a[C:], 0.0) for c, a in zip(chains, a4)]
    x = a_ab
    pw = a_ab
    pwb = [blk(t) for t in pw]
    for _ in range(int(math.log2(C)) - 1):
        pw = [_dot(bf(t), tb) for t, tb in zip(pw, pwb)]
        pwb = [blk(t) for t in pw]
        x = [xi + t + _dot(bf(xi), tb) for xi, t, tb in zip(x, pw, pwb)]
    spb = [bf(c["sp"]) for c in chains]
    vb = [blk(c["v"]) for c in chains]
    x1 = [_dot_nt(bf(c["ah"]), s) + _dot(bf(a), v) for c, s, a, v in zip(chains, spb, a_ak, vb)]
    u = [t + _dot(bf(xi), blk(t)) for t, xi in zip(x1, x)]
    y = [_dot_nt(bf(c["rh"]), s) + _dot(bf(a), cat([blk(ui), v]))
         for c, s, a, ui, v in zip(chains, spb, a_r, u, vb)]
    upd = [_dot_tn(cat([bf(ui), bf(c["v"])]), cat([bf(c["bb"]), bf(c["kb"])])) for c, ui in zip(chains, u)]
    sp_new = [c["sp"] * c["dtot"] + jnp.where(bdmask, t, 0.0) for c, t in zip(chains, upd)]
    return list(zip(y, sp_new))


def _split3(t):
    hi = t.astype(BF16)
    r1 = t - hi.astype(F32)
    mid = r1.astype(BF16)
    lo = (r1 - mid.astype(F32)).astype(BF16)
    return hi, mid, lo


def _delta_prep(r, lw, k, v, kk, b, rev):
    C = CHUNK

    def causal(width, strict):
        ti = lax.broadcasted_iota(jnp.int32, (C, width), 0)
        ii = lax.broadcasted_iota(jnp.int32, (C, width), 1) & (C - 1)
        if rev:
            return ii > ti if strict else ii >= ti
        return ii < ti if strict else ii <= ti

    tri = causal(C, False).astype(BF16)
    L = sum(_dot(tri, part) for part in _split3(lw))
    last = 0 if rev else C - 1
    ltot = L[last:last + 1, :]
    e_nl = jnp.exp(-L)
    e_t = jnp.exp(ltot - L)
    return dict(ah=-kk * jnp.exp(L - lw), rh=r * jnp.exp(L), bh=b * e_nl, kh=k * e_nl, bb=b * e_t,
                kb=k * e_t, v=v, dtot=jnp.exp(ltot)), causal(2 * C, True), causal(4 * C, False)


def _delta_body(*refs, nb):
    ins, outs, s_ref = refs[:12 * nb], refs[12 * nb:14 * nb], refs[14 * nb]

    @pl.when(pl.program_id(1) == 0)
    def _():
        s_ref[...] = jnp.zeros_like(s_ref)

    C, W = ins[0].shape
    m_a = lax.broadcasted_iota(jnp.int32, (C, LANES), 1) < (LANES // 2)
    br = lax.broadcasted_iota(jnp.int32, (LANES, LANES), 0) < (LANES // 2)
    bc = lax.broadcasted_iota(jnp.int32, (LANES, LANES), 1) < (LANES // 2)
    bdmask = br == bc
    chains, dest = [], []
    for d in range(N_DIR):
        for j in range(nb):
            r_ref, v_ref, kk_ref, lw_ref, k_ref, b_ref = ins[(d * nb + j) * 6:(d * nb + j + 1) * 6]
            ops, strict2, incl4 = _delta_prep(r_ref[...], lw_ref[...], k_ref[...], v_ref[...], kk_ref[...],
                                              b_ref[...], rev=(d == 1))
            for p in range(W // LANES):
                sl = slice(p * LANES, (p + 1) * LANES)
                c = {name: t[:, sl] for name, t in ops.items()}
                c.update(sp=s_ref[d, j, p], strict2=strict2, incl4=incl4)
                chains.append(c)
                dest.append((outs[d * nb + j], d, j, p, sl))
    for (y, sp_new), (y_ref, d, j, p, sl) in zip(_delta_chains(chains, m_a, bdmask), dest):
        y_ref[:, sl] = y
        s_ref[d, j, p] = sp_new


def rwkv_scan(r, v, kk, lw, k, b, n_samples, n_lat, n_ctx, samples=2):
    M, W = r.shape
    ncl, ncc = n_lat // CHUNK, n_ctx // CHUNK
    nc = ncl + ncc
    nb = _pick(n_samples, samples)
    ng = n_samples // nb

    def row_block(n_s, s, t):
        return jnp.where(t < ncc, n_s * ncl + s * ncc + t, s * ncl + (t - ncc))

    def step_chunk(d, c):
        if d == 0:
            return c
        return jnp.where(c < ncc, ncc - 1 - c, nc + ncc - 1 - c)

    in_specs, out_specs, args = [], [], []
    for d in range(N_DIR):
        for j in range(nb):
            def rows(bi, c, d=d, j=j):
                return row_block(n_samples, j * ng + bi, step_chunk(d, c))

            def out_rows(bi, c, d=d):
                return row_block(ng, bi, step_chunk(d, c))
            shared = pl.BlockSpec((CHUNK, W), lambda bi, c, rows=rows: (rows(bi, c), 0))
            per_dir = pl.BlockSpec((None, CHUNK, W), lambda bi, c, rows=rows, d=d: (d, rows(bi, c), 0))
            in_specs += [shared] * 3 + [per_dir] * 3
            args += [r, v, kk, lw, k, b]
            out_specs.append(pl.BlockSpec((CHUNK, W), lambda bi, c, out_rows=out_rows: (out_rows(bi, c), 0)))
    outs = pl.pallas_call(
        functools.partial(_delta_body, nb=nb),
        grid=(ng, nc),
        in_specs=in_specs,
        out_specs=out_specs,
        out_shape=[jax.ShapeDtypeStruct((M // nb, W), F32)] * (N_DIR * nb),
        scratch_shapes=[pltpu.VMEM((N_DIR, nb, W // LANES, LANES, LANES), F32)],
        compiler_params=_params("parallel", "arbitrary"),
        name="delta_scan",
    )(*args)
    n_l = ng * n_lat
    y_dir = []
    for d in range(N_DIR):
        slots = outs[d * nb:(d + 1) * nb]
        y_dir.append(jnp.concatenate([t[:n_l] for t in slots] + [t[n_l:] for t in slots], axis=0))
    return y_dir


def _attn_body(*refs, n_seg):
    q_ref, kv, o_ref = refs[0], refs[1:1 + 2 * n_seg], refs[1 + 2 * n_seg]
    outs = []
    for hh in range(2):
        sl = slice(hh * QK_PAD, (hh + 1) * QK_PAD)
        q = q_ref[:, sl]
        s = [_dot_nt(q, kv[2 * g][:, sl]) for g in range(n_seg)]
        m = functools.reduce(jnp.maximum, [jnp.max(t, axis=-1, keepdims=True) for t in s])
        p = [jnp.exp(t - m) for t in s]
        l = sum(jnp.sum(t, axis=-1, keepdims=True) for t in p)
        o = sum(_dot(p[g].astype(BF16), kv[2 * g + 1][...]) for g in range(n_seg))
        outs.append(o / l)
    lane = lax.broadcasted_iota(jnp.int32, outs[0].shape, 1)
    o_ref[...] = jnp.where(lane < V_HEAD, outs[0], outs[1]).astype(o_ref.dtype)


def attention(q, k, v, n_samples, q_row0, n_q, segments, tq=256):
    H = q.shape[1] // QK_PAD
    tq = _pick(n_q, tq)
    assert q_row0 % tq == 0
    nq = n_q // tq
    in_specs = [pl.BlockSpec((tq, 2 * QK_PAD), lambda b, h, i: (q_row0 // tq + b * nq + i, h))]
    args = [q]
    for row0, n in segments:
        assert row0 % n == 0
        in_specs += [pl.BlockSpec((n, 2 * QK_PAD), lambda b, h, i, row0=row0, n=n: (row0 // n + b, h)),
                     pl.BlockSpec((n, 2 * V_HEAD), lambda b, h, i, row0=row0, n=n: (row0 // n + b, h))]
        args += [k, v]
    return pl.pallas_call(
        functools.partial(_attn_body, n_seg=len(segments)),
        grid=(n_samples, H // 2, nq),
        in_specs=in_specs,
        out_specs=pl.BlockSpec((tq, 2 * V_HEAD), lambda b, h, i: (b * nq + i, h)),
        out_shape=jax.ShapeDtypeStruct((n_samples * n_q, H * V_HEAD), BF16),
        compiler_params=_params("parallel", "parallel", "parallel"),
        name="attention",
    )(*args)


def _rms_norm(x, g):
    y = x * lax.rsqrt(jnp.mean(x * x, axis=-1, keepdims=True) + RMS_EPS)
    return y * g


def _shift_mix(z, mu_prev, mu_next):
    z_prev = jnp.pad(z, ((0, 0), (1, 0), (0, 0)))[:, :-1]
    z_next = jnp.pad(z, ((0, 0), (0, 1), (0, 0)))[:, 1:]
    return z + (z_prev - z) * mu_prev + (z_next - z) * mu_next


def _rope_tables(n_tokens):
    rows = n_tokens // GRID_W
    row = jnp.broadcast_to(jnp.arange(rows)[:, None], (rows, GRID_W)).reshape(-1).astype(F32)
    col = jnp.broadcast_to(jnp.arange(GRID_W)[None, :], (rows, GRID_W)).reshape(-1).astype(F32)
    inv_freq = ROPE_THETA ** (-jnp.arange(0, ROPE_AXIS, 2, dtype=F32) / ROPE_AXIS)
    return row[:, None] * inv_freq, col[:, None] * inv_freq


def _rotate_half(t, ang):
    t1, t2 = jnp.split(t, 2, axis=-1)
    cos = jnp.cos(ang)[:, None, :]
    sin = jnp.sin(ang)[:, None, :]
    return jnp.concatenate([t1 * cos - t2 * sin, t2 * cos + t1 * sin], axis=-1)


def _qk_finish(t, gain, angles, n_samples, n_lat, scale):
    M, H, _ = t.shape
    ms = jnp.sum(t * t, axis=-1, keepdims=True) * (1.0 / QK_HEAD)
    t = t * lax.rsqrt(ms + RMS_EPS)
    t = t[..., :QK_HEAD] * gain
    tx = t[:n_samples * n_lat].reshape(n_samples, n_lat, H, QK_HEAD)
    ang_row, ang_col = angles
    t_nope, t_row, t_col = jnp.split(tx, (QK_NOPE, QK_NOPE + ROPE_AXIS), axis=-1)
    tx = jnp.concatenate([t_nope, _rotate_half(t_row, ang_row), _rotate_half(t_col, ang_col)], axis=-1)
    t = jnp.concatenate([tx.reshape(n_samples * n_lat, H, QK_HEAD), t[n_samples * n_lat:]], axis=0) * scale
    t = jnp.pad(t, ((0, 0), (0, 0), (0, QK_PAD - QK_HEAD)))
    return t.reshape(M, H * QK_PAD).astype(BF16)


def _dft_time_matrix(n):
    idx = jnp.arange(n, dtype=jnp.int32)
    kt = (idx[:, None] * idx[None, :]) % n
    ang = kt.astype(F32) * (2.0 * math.pi / n)
    return jnp.concatenate([jnp.cos(ang), -jnp.sin(ang)], axis=1).astype(BF16)


def _dft_channel_matrix():
    idx = jnp.arange(FNO_GROUP_DIM, dtype=jnp.int32)
    ang = ((idx[:, None] * idx[None, :]) % FNO_GROUP_DIM).astype(F32) * (2.0 * math.pi / FNO_GROUP_DIM)
    eye = jnp.eye(FNO_GROUPS, dtype=F32)
    return jnp.concatenate([jnp.kron(eye, jnp.cos(ang)), jnp.kron(eye, jnp.sin(ang))], axis=1)


def _layer(xg, c_act, cctx_act, p, last, consts, dims):
    B, S, Sc = dims
    M, D = xg.shape
    ML = B * S
    dft_x, dft_c, dft_ch, angles = consts

    act = jnp.concatenate([c_act, cctx_act[None, :], jnp.zeros((7 - B, D), F32)], axis=0)
    modtab = matmul_f32_bias(act, p["w_mod"], p["b_mod"])[:B + 1].reshape(B + 1, 6, D)

    w_in = p["w_in"]
    o1 = N_RWKV_IN
    o2 = o1 + N_MLA_IN
    o3 = o2 + FNO_WIDTH
    w_fd = matmul_f32_bias(w_in[:, o2:o3], dft_ch, jnp.zeros((2 * FNO_WIDTH,), F32))
    w_all = jnp.concatenate(
        [w_in[:, :o1], jnp.zeros((D, ZR_PAD - N_RWKV_IN), F32),
         w_in[:, o1:o2], jnp.zeros((D, ZM_PAD - N_MLA_IN), F32),
         w_in[:, o3:], w_fd], axis=1).astype(BF16)
    zr, zm, zg, zd = inproj(xg, modtab, p["g_norm1"], w_all, S)

    W = RWKV_WIDTH
    zr = zr[:, :N_RWKV_IN]
    zr = jnp.concatenate(
        [_shift_mix(zr[:ML].reshape(B, S, N_RWKV_IN), p["mu_prev"], p["mu_next"]).reshape(ML, N_RWKV_IN),
         _shift_mix(zr[ML:].reshape(B, Sc, N_RWKV_IN), p["mu_prev"], p["mu_next"]).reshape(M - ML, N_RWKV_IN)],
        axis=0)
    r, k, v = zr[:, :W], zr[:, W:2 * W], zr[:, 2 * W:3 * W]
    zw = jnp.tanh(zr[:, 3 * W:3 * W + 128]).reshape(M, N_DIR, DECAY_LORA)
    za = zr[:, 3 * W + 128:3 * W + 256].reshape(M, N_DIR, AAA_LORA)
    zgate = zr[:, 3 * W + 256:]
    lw_d, a_d = [], []
    for d in range(N_DIR):
        w_pre = p["w0"][d] + matmul(zw[:, d], p["w2"][d])
        lw_d.append((-math.exp(-0.5)) * jax.nn.sigmoid(w_pre))
        a_d.append(jax.nn.sigmoid(p["a0"][d] + matmul(za[:, d], p["a2"][d])))
    g_out = matmul(jax.nn.sigmoid(zgate), p["g2"])
    kkf = (k * p["k_k"]).reshape(M, RWKV_HEADS, RWKV_HEAD_DIM)
    kkf = kkf / jnp.maximum(jnp.sqrt(jnp.sum(kkf * kkf, axis=-1, keepdims=True)), 1e-12)
    kkf = kkf.reshape(M, W)
    k_dir = jnp.stack([k * (1.0 + (a_d[d] - 1.0) * p["k_a"]) for d in range(N_DIR)])
    b_dir = jnp.stack([kkf * a_d[d] for d in range(N_DIR)])
    y_dir = rwkv_scan(r, v, kkf, jnp.stack(lw_d), k_dir, b_dir, B, S, Sc)
    y = (y_dir[0] + y_dir[1]).reshape(M, RWKV_HEADS, RWKV_HEAD_DIM)
    k_sum = k_dir[0] + k_dir[1]
    hd = lambda t: t.reshape(M, RWKV_HEADS, RWKV_HEAD_DIM)
    mu = jnp.mean(y, axis=-1, keepdims=True)
    var = jnp.mean(jnp.square(y - mu), axis=-1, keepdims=True)
    yn = ((y - mu) * lax.rsqrt(var + GN_EPS)).reshape(M, W) * p["gn_w"] + p["gn_b"]
    bonus = jnp.sum(hd(r) * p["r_k"] * hd(k_sum), axis=-1, keepdims=True) * hd(v)
    o_rwkv = ((yn + bonus.reshape(M, W)) * g_out).astype(BF16)

    c_q, c_kv, k_rope = zm[:, :Q_LORA], zm[:, Q_LORA:Q_LORA + KV_LORA], zm[:, Q_LORA + KV_LORA:N_MLA_IN]
    hpad = ((0, 0), (0, 0), (0, QK_PAD - QK_HEAD))
    w_uq = jnp.pad(p["w_uq"].reshape(Q_LORA, MLA_HEADS, QK_HEAD), hpad).reshape(Q_LORA, MLA_HEADS * QK_PAD)
    w_ukv = p["w_ukv"].reshape(KV_LORA, MLA_HEADS, QK_NOPE + V_HEAD)
    w_uk = w_ukv[..., :QK_NOPE].reshape(KV_LORA, MLA_HEADS * QK_NOPE)
    w_uv = w_ukv[..., QK_NOPE:].reshape(KV_LORA, MLA_HEADS * V_HEAD)
    cq_n = _rms_norm(c_q, p["g_cq"])
    ckv_n = _rms_norm(c_kv, p["g_ckv"]).astype(BF16)
    q = matmul(cq_n, w_uq).reshape(M, MLA_HEADS, QK_PAD)
    k_nope = matmul(ckv_n, w_uk).reshape(M, MLA_HEADS, QK_NOPE)
    vh = matmul(ckv_n, w_uv, out_dtype=BF16)
    k_att = jnp.concatenate(
        [k_nope, jnp.broadcast_to(k_rope[:, None, :], (M, MLA_HEADS, QK_ROPE)),
         jnp.zeros((M, MLA_HEADS, QK_PAD - QK_HEAD), F32)], axis=-1)
    qh = _qk_finish(q, p["g_qn"], angles, B, S, QK_HEAD ** -0.5)
    kh = _qk_finish(k_att, p["g_kn"], angles, B, S, 1.0)
    att = attention(qh, kh, vh, B, 0, S, [(0, S), (ML, Sc)])

    yf = time_dft(dft_x, zd, B, 0, 1.0 / math.sqrt(S * FNO_GROUP_DIM))

    n_rows = ML if last else M
    if not last:
        att = jnp.concatenate([att, attention(qh, kh, vh, B, ML, Sc, [(ML, Sc)])], axis=0)
        yf = jnp.concatenate([yf, time_dft(dft_c, zd, B, ML, 1.0 / math.sqrt(Sc * FNO_GROUP_DIM))], axis=0)
    x_mid = post(o_rwkv, att, yf, zg, xg, modtab, p["w_rwkv_o"], p["w_mla_o"], p["w_fno"], p["w_out"],
                 n_rows, S)
    return ffn(x_mid, modtab, p["g_norm2"], p["w_ffn_in"].astype(BF16), p["w_ffn_out"].astype(BF16), S)


def kernel(x, c, ctx, c_ctx, w_mod, b_mod, g_norm1, w_in, mu_prev, mu_next, w0, w2, a0, a2, k_k, k_a, r_k, g2, gn_w, gn_b, w_rwkv_o, g_cq, g_ckv, w_uq, w_ukv, g_qn, g_kn, w_mla_o, w_fno, w_out, g_norm2, w_ffn_in, w_ffn_out):
    layer_params = dict(w_mod=w_mod, b_mod=b_mod, g_norm1=g_norm1, w_in=w_in, mu_prev=mu_prev,
                        mu_next=mu_next, w0=w0, w2=w2, a0=a0, a2=a2, k_k=k_k, k_a=k_a, r_k=r_k,
                        g2=g2, gn_w=gn_w, gn_b=gn_b, w_rwkv_o=w_rwkv_o, g_cq=g_cq, g_ckv=g_ckv,
                        w_uq=w_uq, w_ukv=w_ukv, g_qn=g_qn, g_kn=g_kn, w_mla_o=w_mla_o, w_fno=w_fno,
                        w_out=w_out, g_norm2=g_norm2, w_ffn_in=w_ffn_in, w_ffn_out=w_ffn_out)
    depth = w_mod.shape[0]
    B, S, D = x.shape
    Sc = ctx.shape[1]
    assert S % ROW_TILE == 0 and (B * Sc) % ROW_TILE == 0 and B <= 7
    consts = (_dft_time_matrix(S), _dft_time_matrix(Sc), _dft_channel_matrix(), _rope_tables(S))
    c_act = jax.nn.silu(c)
    cctx_act = jax.nn.silu(c_ctx)
    xg = jnp.concatenate([x.reshape(B * S, D), ctx.reshape(B * Sc, D)], axis=0)
    for layer in range(depth):
        p = {name: arr[layer] for name, arr in layer_params.items()}
        xg = _layer(xg, c_act, cctx_act, p, layer == depth - 1, consts, (B, S, Sc))
    return xg[:B * S].reshape(B, S, D)
```

```python
import functools
import math

import jax
import jax.numpy as jnp
from jax import lax
from jax.experimental import pallas as pl
from jax.experimental.pallas import tpu as pltpu

F32 = jnp.float32
BF16 = jnp.bfloat16

D_MODEL = 1024
GRID_W = 64
RWKV_HEADS = 8
RWKV_HEAD_DIM = 64
RWKV_WIDTH = RWKV_HEADS * RWKV_HEAD_DIM
DECAY_LORA = 64
AAA_LORA = 64
GATE_LORA = 128
N_DIR = 2
GN_EPS = 64e-5
MLA_HEADS = 8
QK_NOPE = 64
QK_ROPE = 32
QK_HEAD = QK_NOPE + QK_ROPE
V_HEAD = 64
Q_LORA = 384
KV_LORA = 128
ROPE_AXIS = QK_ROPE // 2
ROPE_THETA = 10000.0
FNO_GROUPS = 8
FNO_GROUP_DIM = 64
FNO_WIDTH = FNO_GROUPS * FNO_GROUP_DIM
N_RWKV_IN = 3 * RWKV_WIDTH + N_DIR * DECAY_LORA + N_DIR * AAA_LORA + GATE_LORA
N_MLA_IN = Q_LORA + KV_LORA + QK_ROPE
RMS_EPS = 1e-6

LANES = 128
CHUNK = 64
QK_PAD = LANES
ROW_TILE = 1024
ZR_PAD = 2048
ZM_PAD = 1024
COL_TILE = 1024
FFN_CHUNK = 256
VMEM_LIMIT = 48 * 1024 * 1024


def _params(*sem):
    return pltpu.CompilerParams(dimension_semantics=sem, vmem_limit_bytes=VMEM_LIMIT)


def _dot(a, b):
    return jnp.dot(a, b, preferred_element_type=F32)


def _dot_nt(a, b):
    return lax.dot_general(a, b, (((1,), (1,)), ((), ())), preferred_element_type=F32)


def _dot_tn(a, b):
    return lax.dot_general(a, b, (((0,), (0,)), ((), ())), preferred_element_type=F32)


def _sigmoid(x):
    return 1.0 / (1.0 + jnp.exp(-x))


def _pick(dim, pref):
    t = min(dim, pref)
    while dim % t:
        t //= 2
    return t


def _mm_body(a_ref, b_ref, o_ref):
    o_ref[...] = _dot(a_ref[...], b_ref[...]).astype(o_ref.dtype)


def matmul(a, b, out_dtype=F32, tm=1024, tn=1024):
    a = a.astype(BF16)
    b = b.astype(BF16)
    M, K = a.shape
    _, N = b.shape
    tm, tn = _pick(M, tm), _pick(N, tn)
    return pl.pallas_call(
        _mm_body,
        grid=(M // tm, N // tn),
        in_specs=[pl.BlockSpec((tm, K), lambda i, j: (i, 0)),
                  pl.BlockSpec((K, tn), lambda i, j: (0, j))],
        out_specs=pl.BlockSpec((tm, tn), lambda i, j: (i, j)),
        out_shape=jax.ShapeDtypeStruct((M, N), out_dtype),
        compiler_params=_params("parallel", "parallel"),
        name="mm",
    )(a, b)


def _mm_f32_body(a_ref, b_ref, bias_ref, o_ref):
    o_ref[...] = jnp.dot(a_ref[...], b_ref[...], preferred_element_type=F32,
                         precision=lax.Precision.HIGHEST) + bias_ref[...]


def matmul_f32_bias(a, b, bias, tn=512):
    M, K = a.shape
    _, N = b.shape
    return pl.pallas_call(
        _mm_f32_body,
        grid=(N // tn,),
        in_specs=[pl.BlockSpec((M, K), lambda j: (0, 0)),
                  pl.BlockSpec((K, tn), lambda j: (0, j)),
                  pl.BlockSpec((1, tn), lambda j: (0, j))],
        out_specs=pl.BlockSpec((M, tn), lambda j: (0, j)),
        out_shape=jax.ShapeDtypeStruct((M, N), F32),
        compiler_params=_params("parallel"),
        name="mm_f32_bias",
    )(a, b, bias.reshape(1, N))


def _norm_mod(x, gain, shift, scale):
    y = x * lax.rsqrt(jnp.mean(x * x, axis=-1, keepdims=True) + RMS_EPS)
    return (y * gain) * (1.0 + scale) + shift


def _inproj_body(x_ref, mod_ref, g_ref, w_ref, zr_ref, zm_ref, zg_ref, zd_ref, h_ref):
    j = pl.program_id(1)

    @pl.when(j == 0)
    def _():
        h_ref[...] = _norm_mod(x_ref[...], g_ref[...], mod_ref[0:1, :], mod_ref[1:2, :]).astype(BF16)

    acc = _dot(h_ref[...], w_ref[...])

    @pl.when(j < 2)
    def _():
        zr_ref[...] = acc

    @pl.when(j == 2)
    def _():
        zm_ref[...] = acc

    @pl.when((j >= 3) & (j < 6))
    def _():
        zg_ref[...] = _sigmoid(acc).astype(BF16)

    @pl.when(j == 6)
    def _():
        zd_ref[...] = acc.astype(BF16)


def _mod_row(i, tiles_per_sample, n_samples):
    return jnp.minimum(i // tiles_per_sample, n_samples)


def inproj(xg, modtab, gain, w_all, n_lat):
    M, D = xg.shape
    n_samples = modtab.shape[0] - 1
    tps = n_lat // ROW_TILE
    nj = w_all.shape[1] // COL_TILE
    assert nj == 7
    clamp = lambda j, lo, hi: jnp.minimum(jnp.maximum(j, lo), hi)
    return pl.pallas_call(
        _inproj_body,
        grid=(M // ROW_TILE, nj),
        in_specs=[pl.BlockSpec((ROW_TILE, D), lambda i, j: (i, 0)),
                  pl.BlockSpec((None, 6, D), lambda i, j: (_mod_row(i, tps, n_samples), 0, 0)),
                  pl.BlockSpec((1, D), lambda i, j: (0, 0)),
                  pl.BlockSpec((D, COL_TILE), lambda i, j: (0, j))],
        out_specs=[pl.BlockSpec((ROW_TILE, COL_TILE), lambda i, j: (i, clamp(j, 0, 1))),
                   pl.BlockSpec((ROW_TILE, COL_TILE), lambda i, j: (i, 0)),
                   pl.BlockSpec((ROW_TILE, COL_TILE), lambda i, j: (i, clamp(j - 3, 0, 2))),
                   pl.BlockSpec((ROW_TILE, COL_TILE), lambda i, j: (i, 0))],
        out_shape=[jax.ShapeDtypeStruct((M, ZR_PAD), F32),
                   jax.ShapeDtypeStruct((M, ZM_PAD), F32),
                   jax.ShapeDtypeStruct((M, 3 * D), BF16),
                   jax.ShapeDtypeStruct((M, 2 * FNO_WIDTH), BF16)],
        scratch_shapes=[pltpu.VMEM((ROW_TILE, D), BF16)],
        compiler_params=_params("parallel", "arbitrary"),
        name="inproj",
    )(xg, modtab, gain.reshape(1, D), w_all)


def _head_sum(t, hs, terms=1):
    parts = _split3(t)[:terms]
    return sum(_dot(part, hs) for part in parts)


def _post_body(yf_ref, yb_ref, bonus_ref, gate_ref, gnw_ref, gnb_ref, hs_ref, att_ref, four_ref, zg_ref,
               x_ref, mod_ref, wr_ref, wm_ref, wf_ref, wo_ref, out_ref):
    D = x_ref.shape[1]
    y = yf_ref[...] + yb_ref[...]
    hs = hs_ref[...]
    inv_n = 1.0 / RWKV_HEAD_DIM
    dev = y - _head_sum(y, hs, terms=2) * inv_n
    var = _head_sum(dev * dev, hs) * inv_n
    yn = dev * lax.rsqrt(var + GN_EPS) * gnw_ref[...] + gnb_ref[...]
    o = ((yn + bonus_ref[...]) * gate_ref[...]).astype(BF16)
    a = _dot(o, wr_ref[...])
    b = _dot(att_ref[...], wm_ref[...])
    f = _dot(four_ref[...], wf_ref[...])
    merged = (zg_ref[:, 0:D].astype(F32) * a + zg_ref[:, D:2 * D].astype(F32) * b
              + zg_ref[:, 2 * D:3 * D].astype(F32) * f)
    out_ref[...] = x_ref[...] + mod_ref[2:3, :] * _dot(merged.astype(BF16), wo_ref[...])


def post(y_f, y_b, bonus, gate, gn_w, gn_b, hs, att, four, zg, xg, modtab, w_r, w_m, w_f, w_o,
         n_rows, n_lat, tm=512):
    D = xg.shape[1]
    W = y_f.shape[1]
    n_samples = modtab.shape[0] - 1
    tps = n_lat // tm
    row = lambda i: (i, 0)
    whole = lambda i: (0, 0)
    rows = lambda width: pl.BlockSpec((tm, width), row)
    wspec = pl.BlockSpec((W, D), whole)
    vec = pl.BlockSpec((1, W), whole)
    return pl.pallas_call(
        _post_body,
        grid=(n_rows // tm,),
        in_specs=[rows(W), rows(W), rows(W), rows(W), vec, vec, pl.BlockSpec((W, W), whole),
                  rows(att.shape[1]), rows(four.shape[1]), rows(3 * D), rows(D),
                  pl.BlockSpec((None, 6, D), lambda i: (_mod_row(i, tps, n_samples), 0, 0)),
                  wspec, wspec, wspec, pl.BlockSpec((D, D), whole)],
        out_specs=rows(D),
        out_shape=jax.ShapeDtypeStruct((n_rows, D), F32),
        compiler_params=_params("parallel"),
        name="post",
    )(y_f, y_b, bonus, gate, gn_w.reshape(1, W), gn_b.reshape(1, W), hs, att, four, zg, xg, modtab,
      w_r.astype(BF16), w_m.astype(BF16), w_f.astype(BF16), w_o.astype(BF16))


def _feat_body(z_ref, zp_ref, zn_ref, mup_ref, mun_ref, w2_ref, a2_ref, g2_ref, w0_ref, a0_ref, kk_ref,
               ka_ref, rk_ref, hs_ref, r_ref, v_ref, kkn_ref, lw_ref, kd_ref, bd_ref, gate_ref, bonus_ref,
               *, n_lat_rows, seg_lat, seg_ctx):
    tm = z_ref.shape[0]
    nz = N_RWKV_IN
    W = RWKV_WIDTH
    i = pl.program_id(0)
    z = z_ref[:, :nz]
    loc = lax.broadcasted_iota(jnp.int32, (tm, 1), 0)
    row = loc + i * tm
    is_lat = row < n_lat_rows
    pos = jnp.where(is_lat, row & (seg_lat - 1), (row - n_lat_rows) & (seg_ctx - 1))
    seg_last = jnp.where(is_lat, seg_lat - 1, seg_ctx - 1)
    z_prev = jnp.where(loc == 0, zp_ref[7:8, :nz], pltpu.roll(z, 1, 0))
    z_prev = jnp.where(pos == 0, 0.0, z_prev)
    z_next = jnp.where(loc == tm - 1, zn_ref[0:1, :nz], pltpu.roll(z, tm - 1, 0))
    z_next = jnp.where(pos == seg_last, 0.0, z_next)
    z = z + (z_prev - z) * mup_ref[...] + (z_next - z) * mun_ref[...]

    r, k, v = z[:, :W], z[:, W:2 * W], z[:, 2 * W:3 * W]
    zw = jnp.tanh(z[:, 3 * W:3 * W + LANES]).astype(BF16)
    za = z[:, 3 * W + LANES:3 * W + 2 * LANES].astype(BF16)
    zg = _sigmoid(z[:, 3 * W + 2 * LANES:3 * W + 3 * LANES]).astype(BF16)
    lw = (-math.exp(-0.5)) * _sigmoid(_dot(zw, w2_ref[...]) + w0_ref[...])
    a = _sigmoid(_dot(za, a2_ref[...]) + a0_ref[...])
    hs = hs_ref[...]
    kq = k * kk_ref[...]
    kkn = kq * lax.rsqrt(jnp.maximum(_head_sum(kq * kq, hs), 1e-24))
    k_sum = None
    for d in range(N_DIR):
        a_d = a[:, d * W:(d + 1) * W]
        k_d = k * (1.0 + (a_d - 1.0) * ka_ref[...])
        lw_ref[d] = lw[:, d * W:(d + 1) * W]
        kd_ref[d] = k_d
        bd_ref[d] = kkn * a_d
        k_sum = k_d if k_sum is None else k_sum + k_d
    r_ref[...] = r
    v_ref[...] = v
    kkn_ref[...] = kkn
    gate_ref[...] = _dot(zg, g2_ref[...])
    bonus_ref[...] = _head_sum(r * rk_ref[...] * k_sum, hs) * v


def rwkv_features(zr, p, hs, n_samples, n_lat, n_ctx, tm=256):
    M, ZW = zr.shape
    W = RWKV_WIDTH
    assert n_lat & (n_lat - 1) == 0 and n_ctx & (n_ctx - 1) == 0 and n_lat % tm == 0 and n_ctx % tm == 0
    pad = lambda t: t.reshape(1, N_RWKV_IN)
    zero = jnp.zeros((DECAY_LORA, W), F32)
    blockdiag = lambda t: jnp.concatenate(
        [jnp.concatenate([t[0], zero], axis=1), jnp.concatenate([zero, t[1]], axis=1)], axis=0).astype(BF16)
    row = lambda i: (i, 0)
    whole = lambda i: (0, 0)
    halo = tm // 8
    full = lambda t: pl.BlockSpec(t.shape, whole)
    params = [pad(p["mu_prev"]), pad(p["mu_next"]), blockdiag(p["w2"]), blockdiag(p["a2"]),
              p["g2"].astype(BF16), p["w0"].reshape(1, N_DIR * W), p["a0"].reshape(1, N_DIR * W),
              p["k_k"].reshape(1, W), p["k_a"].reshape(1, W), p["r_k"].reshape(1, W), hs]
    one = pl.BlockSpec((tm, W), row)
    two = pl.BlockSpec((N_DIR, tm, W), lambda i: (0, i, 0))
    s1 = jax.ShapeDtypeStruct((M, W), F32)
    s2 = jax.ShapeDtypeStruct((N_DIR, M, W), F32)
    return pl.pallas_call(
        functools.partial(_feat_body, n_lat_rows=n_samples * n_lat, seg_lat=n_lat, seg_ctx=n_ctx),
        grid=(M // tm,),
        in_specs=[pl.BlockSpec((tm, ZW), row),
                  pl.BlockSpec((8, ZW), lambda i: (jnp.maximum(i * halo - 1, 0), 0)),
                  pl.BlockSpec((8, ZW), lambda i: (jnp.minimum((i + 1) * halo, M // 8 - 1), 0))]
                 + [full(t) for t in params],
        out_specs=[one, one, one, two, two, two, one, one],
        out_shape=[s1, s1, s1, s2, s2, s2, s1, s1],
        compiler_params=_params("parallel"),
        name="rwkv_features",
    )(zr, zr, zr, *params)


def _ffn_body(x_ref, mod_ref, g_ref, wg_ref, wu_ref, wo_ref, out_ref, h_ref, acc_ref, *, n_chunks):
    c = pl.program_id(1)

    @pl.when(c == 0)
    def _():
        h_ref[...] = _norm_mod(x_ref[...], g_ref[...], mod_ref[3:4, :], mod_ref[4:5, :]).astype(BF16)
        acc_ref[...] = jnp.zeros_like(acc_ref)

    h = h_ref[...]
    gate = _dot(h, wg_ref[...])
    up = _dot(h, wu_ref[...])
    act = gate * _sigmoid(gate) * up
    acc_ref[...] += _dot(act.astype(BF16), wo_ref[...])

    @pl.when(c == n_chunks - 1)
    def _():
        out_ref[...] = x_ref[...] + mod_ref[5:6, :] * acc_ref[...]


def ffn(xg, modtab, gain, w_in, w_out, n_lat):
    M, D = xg.shape
    hidden = w_out.shape[0]
    n_samples = modtab.shape[0] - 1
    tps = n_lat // ROW_TILE
    n_chunks = hidden // FFN_CHUNK
    return pl.pallas_call(
        functools.partial(_ffn_body, n_chunks=n_chunks),
        grid=(M // ROW_TILE, n_chunks),
        in_specs=[pl.BlockSpec((ROW_TILE, D), lambda i, c: (i, 0)),
                  pl.BlockSpec((None, 6, D), lambda i, c: (_mod_row(i, tps, n_samples), 0, 0)),
                  pl.BlockSpec((1, D), lambda i, c: (0, 0)),
                  pl.BlockSpec((D, FFN_CHUNK), lambda i, c: (0, c)),
                  pl.BlockSpec((D, FFN_CHUNK), lambda i, c: (0, n_chunks + c)),
                  pl.BlockSpec((FFN_CHUNK, D), lambda i, c: (c, 0))],
        out_specs=pl.BlockSpec((ROW_TILE, D), lambda i, c: (i, 0)),
        out_shape=jax.ShapeDtypeStruct((M, D), F32),
        scratch_shapes=[pltpu.VMEM((ROW_TILE, D), BF16), pltpu.VMEM((ROW_TILE, D), F32)],
        compiler_params=_params("parallel", "arbitrary"),
        name="ffn",
    )(xg, modtab, gain.reshape(1, D), w_in, w_in, w_out)


def _dft_body(w_ref, z_ref, o_ref, acc_ref, *, nk, scale):
    k = pl.program_id(2)

    @pl.when(k == 0)
    def _():
        acc_ref[...] = jnp.zeros_like(acc_ref)

    acc_ref[...] += _dot(w_ref[...], z_ref[...])

    @pl.when(k == nk - 1)
    def _():
        o_ref[...] = (acc_ref[...] * scale).astype(o_ref.dtype)


def time_dft(w, z, n_samples, row0, scale, tm=1024, tk=1024):
    n = w.shape[0]
    Wd = z.shape[1] // 2
    tm, tk = _pick(n, tm), _pick(n, tk)
    nkh = n // tk
    nk = 2 * nkh
    assert row0 % tk == 0
    off = row0 // tk
    return pl.pallas_call(
        functools.partial(_dft_body, nk=nk, scale=scale),
        grid=(n_samples, n // tm, nk),
        in_specs=[pl.BlockSpec((tm, tk), lambda b, i, k: (i, k)),
                  pl.BlockSpec((tk, Wd), lambda b, i, k: (off + b * nkh + k % nkh, k // nkh))],
        out_specs=pl.BlockSpec((tm, Wd), lambda b, i, k: (b * (n // tm) + i, 0)),
        out_shape=jax.ShapeDtypeStruct((n_samples * n, Wd), BF16),
        scratch_shapes=[pltpu.VMEM((tm, Wd), F32)],
        compiler_params=_params("parallel", "parallel", "arbitrary"),
        name="time_dft",
    )(w, z)


def _delta_chains(chains, m_a, bdmask):
    C = CHUNK
    bf = lambda t: t.astype(BF16)
    zero = jnp.zeros((), BF16)
    cat = lambda ts: jnp.concatenate(ts, axis=0)

    def blk(t):
        t = bf(t)
        return cat([jnp.where(m_a, t, zero), jnp.where(m_a, zero, t)])

    a4 = [_dot_nt(cat([bf(c["ah"]), bf(c["rh"])]), cat([blk(c["bh"]), blk(c["kh"])])) for c in chains]
    a_ab = [jnp.where(c["strict2"], a[:C, :2 * C], 0.0) for c, a in zip(chains, a4)]
    a_ak = [jnp.where(c["strict2"], a[:C, 2 * C:], 0.0) for c, a in zip(chains, a4)]
    a_r = [jnp.where(c["incl4"], a[C:], 0.0) for c, a in zip(chains, a4)]
    x = a_ab
    pw = a_ab
    pwb = [blk(t) for t in pw]
    for _ in range(int(math.log2(C)) - 1):
        pw = [_dot(bf(t), tb) for t, tb in zip(pw, pwb)]
        pwb = [blk(t) for t in pw]
        x = [xi + t + _dot(bf(xi), tb) for xi, t, tb in zip(x, pw, pwb)]
    spb = [bf(c["sp"]) for c in chains]
    vb = [blk(c["v"]) for c in chains]
    x1 = [_dot_nt(bf(c["ah"]), s) + _dot(bf(a), v) for c, s, a, v in zip(chains, spb, a_ak, vb)]
    u = [t + _dot(bf(xi), blk(t)) for t, xi in zip(x1, x)]
    y = [_dot_nt(bf(c["rh"]), s) + _dot(bf(a), cat([blk(ui), v]))
         for c, s, a, ui, v in zip(chains, spb, a_r, u, vb)]
    upd = [_dot_tn(cat([bf(ui), bf(c["v"])]), cat([bf(c["bb"]), bf(c["kb"])])) for c, ui in zip(chains, u)]
    sp_new = [c["sp"] * c["dtot"] + jnp.where(bdmask, t, 0.0) for c, t in zip(chains, upd)]
    return list(zip(y, sp_new))


def _split3(t):
    hi = t.astype(BF16)
    r1 = t - hi.astype(F32)
    mid = r1.astype(BF16)
    lo = (r1 - mid.astype(F32)).astype(BF16)
    return hi, mid, lo


def _delta_prep(r, lw, k, v, kk, b, rev):
    C = CHUNK

    def causal(width, strict):
        ti = lax.broadcasted_iota(jnp.int32, (C, width), 0)
        ii = lax.broadcasted_iota(jnp.int32, (C, width), 1) & (C - 1)
        if rev:
            return ii > ti if strict else ii >= ti
        return ii < ti if strict else ii <= ti

    tri = causal(C, False).astype(BF16)
    L = sum(_dot(tri, part) for part in _split3(lw))
    last = 0 if rev else C - 1
    ltot = L[last:last + 1, :]
    e_nl = jnp.exp(-L)
    e_t = jnp.exp(ltot - L)
    return dict(ah=-kk * jnp.exp(L - lw), rh=r * jnp.exp(L), bh=b * e_nl, kh=k * e_nl, bb=b * e_t,
                kb=k * e_t, v=v, dtot=jnp.exp(ltot)), causal(2 * C, True), causal(4 * C, False)


def _delta_body(*refs, nb):
    ins, outs, s_ref = refs[:12 * nb], refs[12 * nb:14 * nb], refs[14 * nb]

    @pl.when(pl.program_id(1) == 0)
    def _():
        s_ref[...] = jnp.zeros_like(s_ref)

    C, W = ins[0].shape
    m_a = lax.broadcasted_iota(jnp.int32, (C, LANES), 1) < (LANES // 2)
    br = lax.broadcasted_iota(jnp.int32, (LANES, LANES), 0) < (LANES // 2)
    bc = lax.broadcasted_iota(jnp.int32, (LANES, LANES), 1) < (LANES // 2)
    bdmask = br == bc
    chains, dest = [], []
    for d in range(N_DIR):
        for j in range(nb):
            r_ref, v_ref, kk_ref, lw_ref, k_ref, b_ref = ins[(d * nb + j) * 6:(d * nb + j + 1) * 6]
            ops, strict2, incl4 = _delta_prep(r_ref[...], lw_ref[...], k_ref[...], v_ref[...], kk_ref[...],
                                              b_ref[...], rev=(d == 1))
            for p in range(W // LANES):
                sl = slice(p * LANES, (p + 1) * LANES)
                c = {name: t[:, sl] for name, t in ops.items()}
                c.update(sp=s_ref[d, j, p], strict2=strict2, incl4=incl4)
                chains.append(c)
                dest.append((outs[d * nb + j], d, j, p, sl))
    for (y, sp_new), (y_ref, d, j, p, sl) in zip(_delta_chains(chains, m_a, bdmask), dest):
        y_ref[:, sl] = y
        s_ref[d, j, p] = sp_new


def rwkv_scan(r, v, kk, lw, k, b, n_samples, n_lat, n_ctx, samples=2):
    M, W = r.shape
    ncl, ncc = n_lat // CHUNK, n_ctx // CHUNK
    nc = ncl + ncc
    nb = _pick(n_samples, samples)
    ng = n_samples // nb

    def row_block(n_s, s, t):
        return jnp.where(t < ncc, n_s * ncl + s * ncc + t, s * ncl + (t - ncc))

    def step_chunk(d, c):
        if d == 0:
            return c
        return jnp.where(c < ncc, ncc - 1 - c, nc + ncc - 1 - c)

    in_specs, out_specs, args = [], [], []
    for d in range(N_DIR):
        for j in range(nb):
            def rows(bi, c, d=d, j=j):
                return row_block(n_samples, j * ng + bi, step_chunk(d, c))

            def out_rows(bi, c, d=d):
                return row_block(ng, bi, step_chunk(d, c))
            shared = pl.BlockSpec((CHUNK, W), lambda bi, c, rows=rows: (rows(bi, c), 0))
            per_dir = pl.BlockSpec((None, CHUNK, W), lambda bi, c, rows=rows, d=d: (d, rows(bi, c), 0))
            in_specs += [shared] * 3 + [per_dir] * 3
            args += [r, v, kk, lw, k, b]
            out_specs.append(pl.BlockSpec((CHUNK, W), lambda bi, c, out_rows=out_rows: (out_rows(bi, c), 0)))
    outs = pl.pallas_call(
        functools.partial(_delta_body, nb=nb),
        grid=(ng, nc),
        in_specs=in_specs,
        out_specs=out_specs,
        out_shape=[jax.ShapeDtypeStruct((M // nb, W), F32)] * (N_DIR * nb),
        scratch_shapes=[pltpu.VMEM((N_DIR, nb, W // LANES, LANES, LANES), F32)],
        compiler_params=_params("parallel", "arbitrary"),
        name="delta_scan",
    )(*args)
    n_l = ng * n_lat
    y_dir = []
    for d in range(N_DIR):
        slots = outs[d * nb:(d + 1) * nb]
        y_dir.append(jnp.concatenate([t[:n_l] for t in slots] + [t[n_l:] for t in slots], axis=0))
    return y_dir


VT_ROWS = V_HEAD + 16
ATTN_LOOKAHEAD = 3


def _attn_body(*refs, n_seg, kv_chunk):
    q_ref, kv, o_ref = refs[0], refs[1:1 + 2 * n_seg], refs[1 + 2 * n_seg]
    items = []
    for g in range(n_seg):
        n = kv[2 * g].shape[0]
        ck = min(kv_chunk, n)
        items += [(hh, g, c0, ck) for c0 in range(0, n, ck) for hh in range(2)]
    q = [q_ref[:, hh * QK_PAD:(hh + 1) * QK_PAD] for hh in range(2)]
    m = [None, None]
    acc = [None, None]
    scores = {}
    for idx in range(len(items) + ATTN_LOOKAHEAD):
        if idx < len(items):
            hh, g, c0, ck = items[idx]
            scores[idx] = _dot_nt(kv[2 * g][c0:c0 + ck, hh * QK_PAD:(hh + 1) * QK_PAD], q[hh])
        if idx >= ATTN_LOOKAHEAD:
            hh, g, c0, ck = items[idx - ATTN_LOOKAHEAD]
            st = scores.pop(idx - ATTN_LOOKAHEAD)
            mc = jnp.max(st, axis=0, keepdims=True)
            m_new = mc if m[hh] is None else jnp.maximum(m[hh], mc)
            pt = jnp.exp2(st - m_new).astype(BF16)
            pv = _dot(kv[2 * g + 1][hh * QK_PAD:hh * QK_PAD + VT_ROWS, c0:c0 + ck], pt)
            acc[hh] = pv if m[hh] is None else acc[hh] * jnp.exp2(m[hh] - m_new) + pv
            m[hh] = m_new
    outs = [a[:V_HEAD] / a[V_HEAD:V_HEAD + 1] for a in acc]
    o_ref[...] = jnp.concatenate(outs, axis=0).T.astype(o_ref.dtype)


def attention(q, k, vt, n_samples, q_row0, n_q, segments, tq=256, kv_chunk=512):
    H = q.shape[1] // QK_PAD
    tq = _pick(n_q, tq)
    assert q_row0 % tq == 0
    nq = n_q // tq
    in_specs = [pl.BlockSpec((tq, 2 * QK_PAD), lambda b, h, i: (q_row0 // tq + b * nq + i, h))]
    args = [q]
    for row0, n in segments:
        assert row0 % n == 0
        in_specs += [pl.BlockSpec((n, 2 * QK_PAD), lambda b, h, i, row0=row0, n=n: (row0 // n + b, h)),
                     pl.BlockSpec((2 * QK_PAD, n), lambda b, h, i, row0=row0, n=n: (h, row0 // n + b))]
        args += [k, vt]
    return pl.pallas_call(
        functools.partial(_attn_body, n_seg=len(segments), kv_chunk=kv_chunk),
        grid=(n_samples, H // 2, nq),
        in_specs=in_specs,
        out_specs=pl.BlockSpec((tq, 2 * V_HEAD), lambda b, h, i: (b * nq + i, h)),
        out_shape=jax.ShapeDtypeStruct((n_samples * n_q, H * V_HEAD), BF16),
        compiler_params=_params("parallel", "parallel", "parallel"),
        name="attention",
    )(*args)


def _rms_norm(x, g):
    y = x * lax.rsqrt(jnp.mean(x * x, axis=-1, keepdims=True) + RMS_EPS)
    return y * g


def _shift_mix(z, mu_prev, mu_next):
    z_prev = jnp.pad(z, ((0, 0), (1, 0), (0, 0)))[:, :-1]
    z_next = jnp.pad(z, ((0, 0), (0, 1), (0, 0)))[:, 1:]
    return z + (z_prev - z) * mu_prev + (z_next - z) * mu_next


def _rope_tables(n_tokens):
    rows = n_tokens // GRID_W
    row = jnp.broadcast_to(jnp.arange(rows)[:, None], (rows, GRID_W)).reshape(-1).astype(F32)
    col = jnp.broadcast_to(jnp.arange(GRID_W)[None, :], (rows, GRID_W)).reshape(-1).astype(F32)
    inv_freq = ROPE_THETA ** (-jnp.arange(0, ROPE_AXIS, 2, dtype=F32) / ROPE_AXIS)
    return row[:, None] * inv_freq, col[:, None] * inv_freq


def _rotate_half(t, ang):
    t1, t2 = jnp.split(t, 2, axis=-1)
    cos = jnp.cos(ang)[:, None, :]
    sin = jnp.sin(ang)[:, None, :]
    return jnp.concatenate([t1 * cos - t2 * sin, t2 * cos + t1 * sin], axis=-1)


def _qk_finish(t, gain, angles, n_samples, n_lat, scale):
    M, H, _ = t.shape
    ms = jnp.sum(t * t, axis=-1, keepdims=True) * (1.0 / QK_HEAD)
    t = t * lax.rsqrt(ms + RMS_EPS)
    t = t[..., :QK_HEAD] * gain
    tx = t[:n_samples * n_lat].reshape(n_samples, n_lat, H, QK_HEAD)
    ang_row, ang_col = angles
    t_nope, t_row, t_col = jnp.split(tx, (QK_NOPE, QK_NOPE + ROPE_AXIS), axis=-1)
    tx = jnp.concatenate([t_nope, _rotate_half(t_row, ang_row), _rotate_half(t_col, ang_col)], axis=-1)
    t = jnp.concatenate([tx.reshape(n_samples * n_lat, H, QK_HEAD), t[n_samples * n_lat:]], axis=0) * scale
    t = jnp.pad(t, ((0, 0), (0, 0), (0, QK_PAD - QK_HEAD)))
    return t.reshape(M, H * QK_PAD).astype(BF16)


def _dft_time_matrix(n):
    idx = jnp.arange(n, dtype=jnp.int32)
    kt = (idx[:, None] * idx[None, :]) % n
    ang = kt.astype(F32) * (2.0 * math.pi / n)
    return jnp.concatenate([jnp.cos(ang), -jnp.sin(ang)], axis=1).astype(BF16)


def _dft_channel_matrix():
    idx = jnp.arange(FNO_GROUP_DIM, dtype=jnp.int32)
    ang = ((idx[:, None] * idx[None, :]) % FNO_GROUP_DIM).astype(F32) * (2.0 * math.pi / FNO_GROUP_DIM)
    eye = jnp.eye(FNO_GROUPS, dtype=F32)
    return jnp.concatenate([jnp.kron(eye, jnp.cos(ang)), jnp.kron(eye, jnp.sin(ang))], axis=1)


def _layer(xg, c_act, cctx_act, p, last, consts, dims):
    B, S, Sc = dims
    M, D = xg.shape
    ML = B * S
    dft_x, dft_c, dft_ch, angles, head_ones = consts

    act = jnp.concatenate([c_act, cctx_act[None, :], jnp.zeros((7 - B, D), F32)], axis=0)
    modtab = matmul_f32_bias(act, p["w_mod"], p["b_mod"])[:B + 1].reshape(B + 1, 6, D)

    w_in = p["w_in"]
    o1 = N_RWKV_IN
    o2 = o1 + N_MLA_IN
    o3 = o2 + FNO_WIDTH
    w_fd = matmul_f32_bias(w_in[:, o2:o3], dft_ch, jnp.zeros((2 * FNO_WIDTH,), F32))
    w_all = jnp.concatenate(
        [w_in[:, :o1], jnp.zeros((D, ZR_PAD - N_RWKV_IN), F32),
         w_in[:, o1:o2], jnp.zeros((D, ZM_PAD - N_MLA_IN), F32),
         w_in[:, o3:], w_fd], axis=1).astype(BF16)
    zr, zm, zg, zd = inproj(xg, modtab, p["g_norm1"], w_all, S)

    r, v, kkf, lw_dir, k_dir, b_dir, g_out, bonus = rwkv_features(zr, p, head_ones, B, S, Sc)
    y_f, y_b = rwkv_scan(r, v, kkf, lw_dir, k_dir, b_dir, B, S, Sc)

    c_q, c_kv, k_rope = zm[:, :Q_LORA], zm[:, Q_LORA:Q_LORA + KV_LORA], zm[:, Q_LORA + KV_LORA:N_MLA_IN]
    hpad = ((0, 0), (0, 0), (0, QK_PAD - QK_HEAD))
    w_uq = jnp.pad(p["w_uq"].reshape(Q_LORA, MLA_HEADS, QK_HEAD), hpad).reshape(Q_LORA, MLA_HEADS * QK_PAD)
    w_ukv = p["w_ukv"].reshape(KV_LORA, MLA_HEADS, QK_NOPE + V_HEAD)
    w_uk = w_ukv[..., :QK_NOPE].reshape(KV_LORA, MLA_HEADS * QK_NOPE)
    w_uv = w_ukv[..., QK_NOPE:].reshape(KV_LORA, MLA_HEADS * V_HEAD)
    cq_n = _rms_norm(c_q, p["g_cq"])
    ckv_n = _rms_norm(c_kv, p["g_ckv"]).astype(BF16)
    q = matmul(cq_n, w_uq).reshape(M, MLA_HEADS, QK_PAD)
    k_nope = matmul(ckv_n, w_uk).reshape(M, MLA_HEADS, QK_NOPE)
    vh = matmul(ckv_n, w_uv, out_dtype=BF16).reshape(M, MLA_HEADS, V_HEAD)
    vh = jnp.concatenate([vh, jnp.ones((M, MLA_HEADS, QK_PAD - V_HEAD), BF16)], axis=-1)
    vh = vh.reshape(M, MLA_HEADS * QK_PAD).T
    k_att = jnp.concatenate(
        [k_nope, jnp.broadcast_to(k_rope[:, None, :], (M, MLA_HEADS, QK_ROPE)),
         jnp.zeros((M, MLA_HEADS, QK_PAD - QK_HEAD), F32)], axis=-1)
    qh = _qk_finish(q, p["g_qn"], angles, B, S, QK_HEAD ** -0.5 * math.log2(math.e))
    kh = _qk_finish(k_att, p["g_kn"], angles, B, S, 1.0)
    att = attention(qh, kh, vh, B, 0, S, [(0, S), (ML, Sc)])

    yf = time_dft(dft_x, zd, B, 0, 1.0 / math.sqrt(S * FNO_GROUP_DIM))

    n_rows = ML if last else M
    if not last:
        att = jnp.concatenate([att, attention(qh, kh, vh, B, ML, Sc, [(ML, Sc)])], axis=0)
        yf = jnp.concatenate([yf, time_dft(dft_c, zd, B, ML, 1.0 / math.sqrt(Sc * FNO_GROUP_DIM))], axis=0)
    x_mid = post(y_f, y_b, bonus, g_out, p["gn_w"], p["gn_b"], head_ones, att, yf, zg, xg, modtab,
                 p["w_rwkv_o"], p["w_mla_o"], p["w_fno"], p["w_out"], n_rows, S)
    return ffn(x_mid, modtab, p["g_norm2"], p["w_ffn_in"].astype(BF16), p["w_ffn_out"].astype(BF16), S)


def kernel(x, c, ctx, c_ctx, w_mod, b_mod, g_norm1, w_in, mu_prev, mu_next, w0, w2, a0, a2, k_k, k_a, r_k, g2, gn_w, gn_b, w_rwkv_o, g_cq, g_ckv, w_uq, w_ukv, g_qn, g_kn, w_mla_o, w_fno, w_out, g_norm2, w_ffn_in, w_ffn_out):
    layer_params = dict(w_mod=w_mod, b_mod=b_mod, g_norm1=g_norm1, w_in=w_in, mu_prev=mu_prev,
                        mu_next=mu_next, w0=w0, w2=w2, a0=a0, a2=a2, k_k=k_k, k_a=k_a, r_k=r_k,
                        g2=g2, gn_w=gn_w, gn_b=gn_b, w_rwkv_o=w_rwkv_o, g_cq=g_cq, g_ckv=g_ckv,
                        w_uq=w_uq, w_ukv=w_ukv, g_qn=g_qn, g_kn=g_kn, w_mla_o=w_mla_o, w_fno=w_fno,
                        w_out=w_out, g_norm2=g_norm2, w_ffn_in=w_ffn_in, w_ffn_out=w_ffn_out)
    depth = w_mod.shape[0]
    B, S, D = x.shape
    Sc = ctx.shape[1]
    assert S % ROW_TILE == 0 and (B * Sc) % ROW_TILE == 0 and B <= 7
    head_ones = jnp.kron(jnp.eye(RWKV_HEADS, dtype=F32),
                         jnp.ones((RWKV_HEAD_DIM, RWKV_HEAD_DIM), F32)).astype(BF16)
    consts = (_dft_time_matrix(S), _dft_time_matrix(Sc), _dft_channel_matrix(), _rope_tables(S), head_ones)
    c_act = jax.nn.silu(c)
    cctx_act = jax.nn.silu(c_ctx)
    xg = jnp.concatenate([x.reshape(B * S, D), ctx.reshape(B * Sc, D)], axis=0)
    for layer in range(depth):
        p = {name: arr[layer] for name, arr in layer_params.items()}
        xg = _layer(xg, c_act, cctx_act, p, layer == depth - 1, consts, (B, S, Sc))
    return xg[:B * S].reshape(B, S, D)
```

```python
import functools
import math

import jax
import jax.numpy as jnp
from jax import lax
from jax.experimental import pallas as pl
from jax.experimental.pallas import tpu as pltpu

F32 = jnp.float32
BF16 = jnp.bfloat16

D_MODEL = 1024
GRID_W = 64
RWKV_HEADS = 8
RWKV_HEAD_DIM = 64
RWKV_WIDTH = RWKV_HEADS * RWKV_HEAD_DIM
DECAY_LORA = 64
AAA_LORA = 64
GATE_LORA = 128
N_DIR = 2
GN_EPS = 64e-5
MLA_HEADS = 8
QK_NOPE = 64
QK_ROPE = 32
QK_HEAD = QK_NOPE + QK_ROPE
V_HEAD = 64
Q_LORA = 384
KV_LORA = 128
ROPE_AXIS = QK_ROPE // 2
ROPE_THETA = 10000.0
FNO_GROUPS = 8
FNO_GROUP_DIM = 64
FNO_WIDTH = FNO_GROUPS * FNO_GROUP_DIM
N_RWKV_IN = 3 * RWKV_WIDTH + N_DIR * DECAY_LORA + N_DIR * AAA_LORA + GATE_LORA
N_MLA_IN = Q_LORA + KV_LORA + QK_ROPE
RMS_EPS = 1e-6

LANES = 128
CHUNK = 64
QK_PAD = LANES
ROW_TILE = 1024
ZR_PAD = 2048
ZM_PAD = 1024
COL_TILE = 1024
FFN_CHUNK = 256
VMEM_LIMIT = 48 * 1024 * 1024


def _params(*sem):
    return pltpu.CompilerParams(dimension_semantics=sem, vmem_limit_bytes=VMEM_LIMIT)


def _dot(a, b):
    return jnp.dot(a, b, preferred_element_type=F32)


def _dot_nt(a, b):
    return lax.dot_general(a, b, (((1,), (1,)), ((), ())), preferred_element_type=F32)


def _dot_tn(a, b):
    return lax.dot_general(a, b, (((0,), (0,)), ((), ())), preferred_element_type=F32)


def _sigmoid(x):
    return 1.0 / (1.0 + jnp.exp(-x))


def _pick(dim, pref):
    t = min(dim, pref)
    while dim % t:
        t //= 2
    return t


def _mm_f32_body(a_ref, b_ref, bias_ref, o_ref):
    o_ref[...] = jnp.dot(a_ref[...], b_ref[...], preferred_element_type=F32,
                         precision=lax.Precision.HIGHEST) + bias_ref[...]


def matmul_f32_bias(a, b, bias, tn=512):
    M, K = a.shape
    _, N = b.shape
    return pl.pallas_call(
        _mm_f32_body,
        grid=(N // tn,),
        in_specs=[pl.BlockSpec((M, K), lambda j: (0, 0)),
                  pl.BlockSpec((K, tn), lambda j: (0, j)),
                  pl.BlockSpec((1, tn), lambda j: (0, j))],
        out_specs=pl.BlockSpec((M, tn), lambda j: (0, j)),
        out_shape=jax.ShapeDtypeStruct((M, N), F32),
        compiler_params=_params("parallel"),
        name="mm_f32_bias",
    )(a, b, bias.reshape(1, N))


def _norm_mod(x, gain, shift, scale):
    y = x * lax.rsqrt(jnp.mean(x * x, axis=-1, keepdims=True) + RMS_EPS)
    return (y * gain) * (1.0 + scale) + shift


def _inproj_body(x_ref, mod_ref, g_ref, wr_ref, wm_ref, wg_ref, wd_ref, zr_ref, zm_ref, zg_ref, zd_ref,
                 h_ref):
    j = pl.program_id(1)

    @pl.when(j == 0)
    def _():
        h_ref[...] = _norm_mod(x_ref[...], g_ref[...], mod_ref[0:1, :], mod_ref[1:2, :]).astype(BF16)

    @pl.when(j < 2)
    def _():
        zr_ref[...] = _dot(h_ref[...], wr_ref[...])

    @pl.when(j == 2)
    def _():
        zm_ref[...] = _dot(h_ref[...], wm_ref[...])

    @pl.when((j >= 3) & (j < 6))
    def _():
        zg_ref[...] = _sigmoid(_dot(h_ref[...], wg_ref[...])).astype(BF16)

    @pl.when(j == 6)
    def _():
        zd_ref[...] = _dot(h_ref[...], wd_ref[...]).astype(BF16)


def _mod_row(i, tiles_per_sample, n_samples):
    return jnp.minimum(i // tiles_per_sample, n_samples)


def inproj(xg, modtab, gain, w_r, w_m, w_g, w_d, n_lat):
    M, D = xg.shape
    n_samples = modtab.shape[0] - 1
    tps = n_lat // ROW_TILE
    clamp = lambda j, lo, hi: jnp.minimum(jnp.maximum(j, lo), hi)
    zr_col = lambda i, j: clamp(j, 0, 1)
    zg_col = lambda i, j: clamp(j - 3, 0, 2)
    first = lambda i, j: 0
    wtile = lambda col: pl.BlockSpec((D, COL_TILE), lambda i, j: (0, col(i, j)))
    otile = lambda col: pl.BlockSpec((ROW_TILE, COL_TILE), lambda i, j: (i, col(i, j)))
    return pl.pallas_call(
        _inproj_body,
        grid=(M // ROW_TILE, 7),
        in_specs=[pl.BlockSpec((ROW_TILE, D), lambda i, j: (i, 0)),
                  pl.BlockSpec((None, 6, D), lambda i, j: (_mod_row(i, tps, n_samples), 0, 0)),
                  pl.BlockSpec((1, D), lambda i, j: (0, 0)),
                  wtile(zr_col), wtile(first), wtile(zg_col), wtile(first)],
        out_specs=[otile(zr_col), otile(first), otile(zg_col), otile(first)],
        out_shape=[jax.ShapeDtypeStruct((M, ZR_PAD), F32),
                   jax.ShapeDtypeStruct((M, ZM_PAD), F32),
                   jax.ShapeDtypeStruct((M, 3 * D), BF16),
                   jax.ShapeDtypeStruct((M, 2 * FNO_WIDTH), BF16)],
        scratch_shapes=[pltpu.VMEM((ROW_TILE, D), BF16)],
        compiler_params=_params("parallel", "arbitrary"),
        name="inproj",
    )(xg, modtab, gain.reshape(1, D), w_r, w_m, w_g, w_d)


def _head_sum(t, hs, terms=1):
    parts = _split3(t)[:terms]
    return sum(_dot(part, hs) for part in parts)


def _slot_tile(i, tm, n_lat, n_ctx, n_samples, nb):
    ng = n_samples // nb
    tps, lat_tiles = n_lat // tm, n_samples * n_lat // tm
    ctx_tiles = ng * n_ctx // tm
    sample = i // tps
    c = i - lat_tiles
    slot = jnp.where(i < lat_tiles, sample // ng, c // ctx_tiles)
    local = jnp.where(i < lat_tiles, (sample % ng) * tps + i % tps, ng * tps + c % ctx_tiles)
    return slot, local


def _post_body(*refs, nb, slot_of):
    y_refs = refs[:2 * nb]
    (bonus_ref, gate_ref, gnw_ref, gnb_ref, hs_ref, att_ref, four_ref, zg_ref, x_ref, mod_ref,
     wr_ref, wm_ref, wf_ref, wo_ref, out_ref) = refs[2 * nb:]
    D = x_ref.shape[1]
    slot = slot_of(pl.program_id(0))
    y = y_refs[0][...] + y_refs[nb][...]
    for j in range(1, nb):
        y = jnp.where(slot == j, y_refs[j][...] + y_refs[nb + j][...], y)
    hs = hs_ref[...]
    inv_n = 1.0 / RWKV_HEAD_DIM
    dev = y - _head_sum(y, hs, terms=2) * inv_n
    var = _head_sum(dev * dev, hs) * inv_n
    yn = dev * lax.rsqrt(var + GN_EPS) * gnw_ref[...] + gnb_ref[...]
    o = ((yn + bonus_ref[...]) * gate_ref[...]).astype(BF16)
    a = _dot(o, wr_ref[...])
    b = _dot(att_ref[...], wm_ref[...])
    f = _dot(four_ref[...], wf_ref[...])
    merged = (zg_ref[:, 0:D].astype(F32) * a + zg_ref[:, D:2 * D].astype(F32) * b
              + zg_ref[:, 2 * D:3 * D].astype(F32) * f)
    out_ref[...] = x_ref[...] + mod_ref[2:3, :] * _dot(merged.astype(BF16), wo_ref[...])


def post(y_slots, bonus, gate, gn_w, gn_b, hs, att, four, zg, xg, modtab, w_r, w_m, w_f, w_o,
         n_rows, n_lat, n_ctx, tm=512):
    D = xg.shape[1]
    W = bonus.shape[1]
    n_samples = modtab.shape[0] - 1
    nb = len(y_slots) // N_DIR
    assert (n_samples // nb * n_ctx) % tm == 0 and n_lat % tm == 0
    tps = n_lat // tm
    slot_tile = functools.partial(_slot_tile, tm=tm, n_lat=n_lat, n_ctx=n_ctx, n_samples=n_samples, nb=nb)
    row = lambda i: (i, 0)
    whole = lambda i: (0, 0)
    rows = lambda width: pl.BlockSpec((tm, width), row)
    wspec = pl.BlockSpec((W, D), whole)
    vec = pl.BlockSpec((1, W), whole)

    def slot_spec(j):
        def index(i):
            slot, local = slot_tile(i)
            return (jnp.where(slot == j, local, 0), 0)
        return pl.BlockSpec((tm, W), index)

    return pl.pallas_call(
        functools.partial(_post_body, nb=nb, slot_of=lambda i: slot_tile(i)[0]),
        grid=(n_rows // tm,),
        in_specs=[slot_spec(j) for _ in range(N_DIR) for j in range(nb)]
                 + [rows(W), rows(W), vec, vec, pl.BlockSpec((W, W), whole),
                    rows(att.shape[1]), rows(four.shape[1]), rows(3 * D), rows(D),
                    pl.BlockSpec((None, 6, D), lambda i: (_mod_row(i, tps, n_samples), 0, 0)),
                    wspec, wspec, wspec, pl.BlockSpec((D, D), whole)],
        out_specs=rows(D),
        out_shape=jax.ShapeDtypeStruct((n_rows, D), F32),
        compiler_params=_params("parallel"),
        name="post",
    )(*y_slots, bonus, gate, gn_w.reshape(1, W), gn_b.reshape(1, W), hs, att, four, zg, xg, modtab,
      w_r.astype(BF16), w_m.astype(BF16), w_f.astype(BF16), w_o.astype(BF16))


def _feat_body(z_ref, zp_ref, zn_ref, mup_ref, mun_ref, w2_ref, a2_ref, g2_ref, w0_ref, a0_ref, kk_ref,
               ka_ref, rk_ref, hs_ref, r_ref, v_ref, kkn_ref, lw_ref, kd_ref, bd_ref, gate_ref, bonus_ref,
               *, n_lat_rows, seg_lat, seg_ctx):
    tm = z_ref.shape[0]
    nz = N_RWKV_IN
    W = RWKV_WIDTH
    i = pl.program_id(0)
    z = z_ref[:, :nz]
    loc = lax.broadcasted_iota(jnp.int32, (tm, 1), 0)
    row = loc + i * tm
    is_lat = row < n_lat_rows
    pos = jnp.where(is_lat, row & (seg_lat - 1), (row - n_lat_rows) & (seg_ctx - 1))
    seg_last = jnp.where(is_lat, seg_lat - 1, seg_ctx - 1)
    z_prev = jnp.where(loc == 0, zp_ref[7:8, :nz], pltpu.roll(z, 1, 0))
    z_prev = jnp.where(pos == 0, 0.0, z_prev)
    z_next = jnp.where(loc == tm - 1, zn_ref[0:1, :nz], pltpu.roll(z, tm - 1, 0))
    z_next = jnp.where(pos == seg_last, 0.0, z_next)
    z = z + (z_prev - z) * mup_ref[...] + (z_next - z) * mun_ref[...]

    r, k, v = z[:, :W], z[:, W:2 * W], z[:, 2 * W:3 * W]
    zw = jnp.tanh(z[:, 3 * W:3 * W + LANES]).astype(BF16)
    za = z[:, 3 * W + LANES:3 * W + 2 * LANES].astype(BF16)
    zg = _sigmoid(z[:, 3 * W + 2 * LANES:3 * W + 3 * LANES]).astype(BF16)
    lw = (-math.exp(-0.5)) * _sigmoid(_dot(zw, w2_ref[...]) + w0_ref[...])
    a = _sigmoid(_dot(za, a2_ref[...]) + a0_ref[...])
    hs = hs_ref[...]
    kq = k * kk_ref[...]
    kkn = kq * lax.rsqrt(jnp.maximum(_head_sum(kq * kq, hs), 1e-24))
    k_sum = None
    for d in range(N_DIR):
        a_d = a[:, d * W:(d + 1) * W]
        k_d = k * (1.0 + (a_d - 1.0) * ka_ref[...])
        lw_ref[d] = lw[:, d * W:(d + 1) * W]
        kd_ref[d] = k_d
        bd_ref[d] = kkn * a_d
        k_sum = k_d if k_sum is None else k_sum + k_d
    r_ref[...] = r
    v_ref[...] = v
    kkn_ref[...] = kkn
    gate_ref[...] = _dot(zg, g2_ref[...])
    bonus_ref[...] = _head_sum(r * rk_ref[...] * k_sum, hs) * v


def rwkv_features(zr, p, hs, n_samples, n_lat, n_ctx, tm=256):
    M, ZW = zr.shape
    W = RWKV_WIDTH
    assert n_lat & (n_lat - 1) == 0 and n_ctx & (n_ctx - 1) == 0 and n_lat % tm == 0 and n_ctx % tm == 0
    pad = lambda t: t.reshape(1, N_RWKV_IN)
    zero = jnp.zeros((DECAY_LORA, W), F32)
    blockdiag = lambda t: jnp.concatenate(
        [jnp.concatenate([t[0], zero], axis=1), jnp.concatenate([zero, t[1]], axis=1)], axis=0).astype(BF16)
    row = lambda i: (i, 0)
    whole = lambda i: (0, 0)
    halo = tm // 8
    full = lambda t: pl.BlockSpec(t.shape, whole)
    params = [pad(p["mu_prev"]), pad(p["mu_next"]), blockdiag(p["w2"]), blockdiag(p["a2"]),
              p["g2"].astype(BF16), p["w0"].reshape(1, N_DIR * W), p["a0"].reshape(1, N_DIR * W),
              p["k_k"].reshape(1, W), p["k_a"].reshape(1, W), p["r_k"].reshape(1, W), hs]
    one = pl.BlockSpec((tm, W), row)
    two = pl.BlockSpec((N_DIR, tm, W), lambda i: (0, i, 0))
    s1 = jax.ShapeDtypeStruct((M, W), F32)
    s2 = jax.ShapeDtypeStruct((N_DIR, M, W), F32)
    return pl.pallas_call(
        functools.partial(_feat_body, n_lat_rows=n_samples * n_lat, seg_lat=n_lat, seg_ctx=n_ctx),
        grid=(M // tm,),
        in_specs=[pl.BlockSpec((tm, ZW), row),
                  pl.BlockSpec((8, ZW), lambda i: (jnp.maximum(i * halo - 1, 0), 0)),
                  pl.BlockSpec((8, ZW), lambda i: (jnp.minimum((i + 1) * halo, M // 8 - 1), 0))]
                 + [full(t) for t in params],
        out_specs=[one, one, one, two, two, two, one, one],
        out_shape=[s1, s1, s1, s2, s2, s2, s1, s1],
        compiler_params=_params("parallel"),
        name="rwkv_features",
    )(zr, zr, zr, *params)


def _ffn_body(x_ref, mod_ref, g_ref, wg_ref, wu_ref, wo_ref, out_ref, h_ref, acc_ref, *, n_chunks):
    c = pl.program_id(1)

    @pl.when(c == 0)
    def _():
        h_ref[...] = _norm_mod(x_ref[...], g_ref[...], mod_ref[3:4, :], mod_ref[4:5, :]).astype(BF16)
        acc_ref[...] = jnp.zeros_like(acc_ref)

    h = h_ref[...]
    gate = _dot(h, wg_ref[...])
    up = _dot(h, wu_ref[...])
    act = gate * _sigmoid(gate) * up
    acc_ref[...] += _dot(act.astype(BF16), wo_ref[...])

    @pl.when(c == n_chunks - 1)
    def _():
        out_ref[...] = x_ref[...] + mod_ref[5:6, :] * acc_ref[...]


def ffn(xg, modtab, gain, w_in, w_out, n_lat):
    M, D = xg.shape
    hidden = w_out.shape[0]
    n_samples = modtab.shape[0] - 1
    tps = n_lat // ROW_TILE
    n_chunks = hidden // FFN_CHUNK
    return pl.pallas_call(
        functools.partial(_ffn_body, n_chunks=n_chunks),
        grid=(M // ROW_TILE, n_chunks),
        in_specs=[pl.BlockSpec((ROW_TILE, D), lambda i, c: (i, 0)),
                  pl.BlockSpec((None, 6, D), lambda i, c: (_mod_row(i, tps, n_samples), 0, 0)),
                  pl.BlockSpec((1, D), lambda i, c: (0, 0)),
                  pl.BlockSpec((D, FFN_CHUNK), lambda i, c: (0, c)),
                  pl.BlockSpec((D, FFN_CHUNK), lambda i, c: (0, n_chunks + c)),
                  pl.BlockSpec((FFN_CHUNK, D), lambda i, c: (c, 0))],
        out_specs=pl.BlockSpec((ROW_TILE, D), lambda i, c: (i, 0)),
        out_shape=jax.ShapeDtypeStruct((M, D), F32),
        scratch_shapes=[pltpu.VMEM((ROW_TILE, D), BF16), pltpu.VMEM((ROW_TILE, D), F32)],
        compiler_params=_params("parallel", "arbitrary"),
        name="ffn",
    )(xg, modtab, gain.reshape(1, D), w_in, w_in, w_out)


def _dft_body(w_ref, z_ref, o_ref, acc_ref, *, nk, scale):
    k = pl.program_id(2)

    @pl.when(k == 0)
    def _():
        acc_ref[...] = jnp.zeros_like(acc_ref)

    acc_ref[...] += _dot(w_ref[...], z_ref[...])

    @pl.when(k == nk - 1)
    def _():
        o_ref[...] = (acc_ref[...] * scale).astype(o_ref.dtype)


def time_dft(w, z, n_samples, row0, scale, tm=1024, tk=1024):
    n = w.shape[0]
    Wd = z.shape[1] // 2
    tm, tk = _pick(n, tm), _pick(n, tk)
    nkh = n // tk
    nk = 2 * nkh
    assert row0 % tk == 0
    off = row0 // tk
    return pl.pallas_call(
        functools.partial(_dft_body, nk=nk, scale=scale),
        grid=(n_samples, n // tm, nk),
        in_specs=[pl.BlockSpec((tm, tk), lambda b, i, k: (i, k)),
                  pl.BlockSpec((tk, Wd), lambda b, i, k: (off + b * nkh + k % nkh, k // nkh))],
        out_specs=pl.BlockSpec((tm, Wd), lambda b, i, k: (b * (n // tm) + i, 0)),
        out_shape=jax.ShapeDtypeStruct((n_samples * n, Wd), BF16),
        scratch_shapes=[pltpu.VMEM((tm, Wd), F32)],
        compiler_params=_params("parallel", "parallel", "arbitrary"),
        name="time_dft",
    )(w, z)


def _delta_chains(chains, m_a, bdmask):
    C = CHUNK
    bf = lambda t: t.astype(BF16)
    zero = jnp.zeros((), BF16)
    cat = lambda ts: jnp.concatenate(ts, axis=0)

    def blk(t):
        t = bf(t)
        return cat([jnp.where(m_a, t, zero), jnp.where(m_a, zero, t)])

    a4 = [_dot_nt(cat([bf(c["ah"]), bf(c["rh"])]), cat([blk(c["bh"]), blk(c["kh"])])) for c in chains]
    a_ab = [jnp.where(c["strict2"], a[:C, :2 * C], 0.0) for c, a in zip(chains, a4)]
    a_ak = [jnp.where(c["strict2"], a[:C, 2 * C:], 0.0) for c, a in zip(chains, a4)]
    a_r = [jnp.where(c["incl4"], a[C:], 0.0) for c, a in zip(chains, a4)]
    x = a_ab
    pw = a_ab
    pwb = [blk(t) for t in pw]
    for _ in range(int(math.log2(C)) - 1):
        pw = [_dot(bf(t), tb) for t, tb in zip(pw, pwb)]
        pwb = [blk(t) for t in pw]
        x = [xi + t + _dot(bf(xi), tb) for xi, t, tb in zip(x, pw, pwb)]
    spb = [bf(c["sp"]) for c in chains]
    vb = [blk(c["v"]) for c in chains]
    x1 = [_dot_nt(bf(c["ah"]), s) + _dot(bf(a), v) for c, s, a, v in zip(chains, spb, a_ak, vb)]
    u = [t + _dot(bf(xi), blk(t)) for t, xi in zip(x1, x)]
    y = [_dot_nt(bf(c["rh"]), s) + _dot(bf(a), cat([blk(ui), v]))
         for c, s, a, ui, v in zip(chains, spb, a_r, u, vb)]
    upd = [_dot_tn(cat([bf(ui), bf(c["v"])]), cat([bf(c["bb"]), bf(c["kb"])])) for c, ui in zip(chains, u)]
    sp_new = [c["sp"] * c["dtot"] + jnp.where(bdmask, t, 0.0) for c, t in zip(chains, upd)]
    return list(zip(y, sp_new))


def _split3(t):
    hi = t.astype(BF16)
    r1 = t - hi.astype(F32)
    mid = r1.astype(BF16)
    lo = (r1 - mid.astype(F32)).astype(BF16)
    return hi, mid, lo


def _delta_prep(r, lw, k, v, kk, b, rev):
    C = CHUNK

    def causal(width, strict):
        ti = lax.broadcasted_iota(jnp.int32, (C, width), 0)
        ii = lax.broadcasted_iota(jnp.int32, (C, width), 1) & (C - 1)
        if rev:
            return ii > ti if strict else ii >= ti
        return ii < ti if strict else ii <= ti

    tri = causal(C, False).astype(BF16)
    L = sum(_dot(tri, part) for part in _split3(lw))
    last = 0 if rev else C - 1
    ltot = L[last:last + 1, :]
    e_nl = jnp.exp(-L)
    e_t = jnp.exp(ltot - L)
    return dict(ah=-kk * jnp.exp(L - lw), rh=r * jnp.exp(L), bh=b * e_nl, kh=k * e_nl, bb=b * e_t,
                kb=k * e_t, v=v, dtot=jnp.exp(ltot)), causal(2 * C, True), causal(4 * C, False)


def _delta_body(*refs, nb):
    ins, outs, s_ref = refs[:12 * nb], refs[12 * nb:14 * nb], refs[14 * nb]

    @pl.when(pl.program_id(1) == 0)
    def _():
        s_ref[...] = jnp.zeros_like(s_ref)

    C, W = ins[0].shape
    m_a = lax.broadcasted_iota(jnp.int32, (C, LANES), 1) < (LANES // 2)
    br = lax.broadcasted_iota(jnp.int32, (LANES, LANES), 0) < (LANES // 2)
    bc = lax.broadcasted_iota(jnp.int32, (LANES, LANES), 1) < (LANES // 2)
    bdmask = br == bc
    chains, dest = [], []
    for d in range(N_DIR):
        for j in range(nb):
            r_ref, v_ref, kk_ref, lw_ref, k_ref, b_ref = ins[(d * nb + j) * 6:(d * nb + j + 1) * 6]
            ops, strict2, incl4 = _delta_prep(r_ref[...], lw_ref[...], k_ref[...], v_ref[...], kk_ref[...],
                                              b_ref[...], rev=(d == 1))
            for p in range(W // LANES):
                sl = slice(p * LANES, (p + 1) * LANES)
                c = {name: t[:, sl] for name, t in ops.items()}
                c.update(sp=s_ref[d, j, p], strict2=strict2, incl4=incl4)
                chains.append(c)
                dest.append((outs[d * nb + j], d, j, p, sl))
    for (y, sp_new), (y_ref, d, j, p, sl) in zip(_delta_chains(chains, m_a, bdmask), dest):
        y_ref[:, sl] = y
        s_ref[d, j, p] = sp_new


def rwkv_scan(r, v, kk, lw, k, b, n_samples, n_lat, n_ctx, samples=2):
    M, W = r.shape
    ncl, ncc = n_lat // CHUNK, n_ctx // CHUNK
    nc = ncl + ncc
    nb = _pick(n_samples, samples)
    ng = n_samples // nb

    def row_block(n_s, s, t):
        return jnp.where(t < ncc, n_s * ncl + s * ncc + t, s * ncl + (t - ncc))

    def step_chunk(d, c):
        if d == 0:
            return c
        return jnp.where(c < ncc, ncc - 1 - c, nc + ncc - 1 - c)

    in_specs, out_specs, args = [], [], []
    for d in range(N_DIR):
        for j in range(nb):
            def rows(bi, c, d=d, j=j):
                return row_block(n_samples, j * ng + bi, step_chunk(d, c))

            def out_rows(bi, c, d=d):
                return row_block(ng, bi, step_chunk(d, c))
            shared = pl.BlockSpec((CHUNK, W), lambda bi, c, rows=rows: (rows(bi, c), 0))
            per_dir = pl.BlockSpec((None, CHUNK, W), lambda bi, c, rows=rows, d=d: (d, rows(bi, c), 0))
            in_specs += [shared] * 3 + [per_dir] * 3
            args += [r, v, kk, lw, k, b]
            out_specs.append(pl.BlockSpec((CHUNK, W), lambda bi, c, out_rows=out_rows: (out_rows(bi, c), 0)))
    outs = pl.pallas_call(
        functools.partial(_delta_body, nb=nb),
        grid=(ng, nc),
        in_specs=in_specs,
        out_specs=out_specs,
        out_shape=[jax.ShapeDtypeStruct((M // nb, W), F32)] * (N_DIR * nb),
        scratch_shapes=[pltpu.VMEM((N_DIR, nb, W // LANES, LANES, LANES), F32)],
        compiler_params=_params("parallel", "arbitrary"),
        name="delta_scan",
    )(*args)
    return outs


VT_ROWS = V_HEAD + 16
ATTN_LOOKAHEAD = 3


def _attn_body(*refs, n_seg, kv_chunk):
    q_ref, kv, o_ref = refs[0], refs[1:1 + 2 * n_seg], refs[1 + 2 * n_seg]
    items = []
    for g in range(n_seg):
        n = kv[2 * g].shape[0]
        ck = min(kv_chunk, n)
        items += [(hh, g, c0, ck) for c0 in range(0, n, ck) for hh in range(2)]
    q = [q_ref[:, hh * QK_PAD:(hh + 1) * QK_PAD] for hh in range(2)]
    m = [None, None]
    acc = [None, None]
    scores = {}
    for idx in range(len(items) + ATTN_LOOKAHEAD):
        if idx < len(items):
            hh, g, c0, ck = items[idx]
            scores[idx] = _dot_nt(kv[2 * g][c0:c0 + ck, hh * QK_PAD:(hh + 1) * QK_PAD], q[hh])
        if idx >= ATTN_LOOKAHEAD:
            hh, g, c0, ck = items[idx - ATTN_LOOKAHEAD]
            st = scores.pop(idx - ATTN_LOOKAHEAD)
            mc = jnp.max(st, axis=0, keepdims=True)
            m_new = mc if m[hh] is None else jnp.maximum(m[hh], mc)
            pt = jnp.exp2(st - m_new).astype(BF16)
            pv = _dot(kv[2 * g + 1][hh * QK_PAD:hh * QK_PAD + VT_ROWS, c0:c0 + ck], pt)
            acc[hh] = pv if m[hh] is None else acc[hh] * jnp.exp2(m[hh] - m_new) + pv
            m[hh] = m_new
    outs = [a[:V_HEAD] / a[V_HEAD:V_HEAD + 1] for a in acc]
    o_ref[...] = jnp.concatenate(outs, axis=0).T.astype(o_ref.dtype)


def attention(q, k, vt, n_samples, q_row0, n_q, segments, tq=256, kv_chunk=512):
    H = q.shape[1] // QK_PAD
    tq = _pick(n_q, tq)
    assert q_row0 % tq == 0
    nq = n_q // tq
    in_specs = [pl.BlockSpec((tq, 2 * QK_PAD), lambda b, h, i: (q_row0 // tq + b * nq + i, h))]
    args = [q]
    for row0, n in segments:
        assert row0 % n == 0
        in_specs += [pl.BlockSpec((n, 2 * QK_PAD), lambda b, h, i, row0=row0, n=n: (row0 // n + b, h)),
                     pl.BlockSpec((2 * QK_PAD, n), lambda b, h, i, row0=row0, n=n: (h, row0 // n + b))]
        args += [k, vt]
    return pl.pallas_call(
        functools.partial(_attn_body, n_seg=len(segments), kv_chunk=kv_chunk),
        grid=(n_samples, H // 2, nq),
        in_specs=in_specs,
        out_specs=pl.BlockSpec((tq, 2 * V_HEAD), lambda b, h, i: (b * nq + i, h)),
        out_shape=jax.ShapeDtypeStruct((n_samples * n_q, H * V_HEAD), BF16),
        compiler_params=_params("parallel", "parallel", "parallel"),
        name="attention",
    )(*args)


def _lane_sum(t, ones):
    tb = t.astype(BF16)
    return jnp.concatenate([_dot(tb[:, c:c + LANES], ones) for c in range(0, t.shape[1], LANES)], axis=1)


def _mla_body(z_ref, gcq_ref, gckv_ref, wq_ref, wk_ref, wv_ref, gq_ref, gk_ref, cos_ref, sin_ref, ones_ref,
              q_ref, k_ref, vt_ref, *, q_scale):
    tm = z_ref.shape[0]
    H = MLA_HEADS
    ones = ones_ref[...]
    c_q = z_ref[:, :Q_LORA]
    c_kv = z_ref[:, Q_LORA:Q_LORA + KV_LORA]
    cq_n = c_q * lax.rsqrt(jnp.mean(c_q * c_q, axis=-1, keepdims=True) + RMS_EPS) * gcq_ref[...]
    ckv_n = (c_kv * lax.rsqrt(jnp.mean(c_kv * c_kv, axis=-1, keepdims=True) + RMS_EPS)
             * gckv_ref[...]).astype(BF16)
    lane = lax.broadcasted_iota(jnp.int32, (tm, LANES), 1)
    k_rope = pltpu.roll(jnp.where(lane < QK_ROPE, z_ref[:, Q_LORA + KV_LORA:Q_LORA + KV_LORA + LANES], 0.0),
                        QK_NOPE, 1)
    q = _dot(cq_n.astype(BF16), wq_ref[...])
    k = _dot(ckv_n, wk_ref[...]) + jnp.concatenate([k_rope] * H, axis=1)
    cos = jnp.concatenate([cos_ref[...]] * H, axis=1)
    sin = jnp.concatenate([sin_ref[...]] * H, axis=1)
    first_half = (lax.broadcasted_iota(jnp.int32, (tm, H * LANES), 1) & (ROPE_AXIS // 2)) == 0

    def finish(t, gain, scale):
        t = t * lax.rsqrt(_lane_sum(t * t, ones) * (1.0 / QK_HEAD) + RMS_EPS) * gain
        partner = jnp.where(first_half, pltpu.roll(t, H * LANES - ROPE_AXIS // 2, 1),
                            pltpu.roll(t, ROPE_AXIS // 2, 1))
        return ((t * cos + partner * sin) * scale).astype(BF16)

    q_ref[...] = finish(q, gq_ref[...], q_scale)
    k_ref[...] = finish(k, gk_ref[...], 1.0)
    v = _dot(ckv_n, wv_ref[...])
    ones_rows = (lax.broadcasted_iota(jnp.int32, (tm, H * LANES), 1) & (LANES - 1)) >= V_HEAD
    vt_ref[...] = jnp.where(ones_rows, 1.0, v).T.astype(BF16)


def mla_prep(zm, p, rope_cos, rope_sin, n_lat_rows, n_lat, tm=256):
    M, ZW = zm.shape
    H = MLA_HEADS
    hpad = lambda t, n: jnp.pad(t, ((0, 0), (0, 0), (0, QK_PAD - n))).reshape(t.shape[0], H * QK_PAD)
    w_uq = hpad(p["w_uq"].reshape(Q_LORA, H, QK_HEAD), QK_HEAD).astype(BF16)
    w_ukv = p["w_ukv"].reshape(KV_LORA, H, QK_NOPE + V_HEAD)
    w_uk = hpad(w_ukv[..., :QK_NOPE], QK_NOPE).astype(BF16)
    w_uv = hpad(w_ukv[..., QK_NOPE:], V_HEAD).astype(BF16)
    gain = lambda g: jnp.tile(jnp.pad(g, (0, QK_PAD - QK_HEAD)), H).reshape(1, H * QK_PAD)
    ones = jnp.ones((LANES, LANES), BF16)
    consts = [p["g_cq"].reshape(1, Q_LORA), p["g_ckv"].reshape(1, KV_LORA), w_uq, w_uk, w_uv,
              gain(p["g_qn"]), gain(p["g_kn"])]
    row = lambda i: (i, 0)
    whole = lambda i: (0, 0)
    full = lambda t: pl.BlockSpec(t.shape, whole)
    lat_tiles, per_sample = n_lat_rows // tm, n_lat // tm
    tab = lambda i: (jnp.where(i < lat_tiles, i % per_sample, per_sample), 0)
    return pl.pallas_call(
        functools.partial(_mla_body, q_scale=QK_HEAD ** -0.5 * math.log2(math.e)),
        grid=(M // tm,),
        in_specs=[pl.BlockSpec((tm, ZW), row)] + [full(t) for t in consts]
                 + [pl.BlockSpec((tm, LANES), tab), pl.BlockSpec((tm, LANES), tab), full(ones)],
        out_specs=[pl.BlockSpec((tm, H * QK_PAD), row), pl.BlockSpec((tm, H * QK_PAD), row),
                   pl.BlockSpec((H * QK_PAD, tm), lambda i: (0, i))],
        out_shape=[jax.ShapeDtypeStruct((M, H * QK_PAD), BF16), jax.ShapeDtypeStruct((M, H * QK_PAD), BF16),
                   jax.ShapeDtypeStruct((H * QK_PAD, M), BF16)],
        compiler_params=_params("parallel"),
        name="mla_prep",
    )(zm, *consts, rope_cos, rope_sin, ones)


def _rope_lane_tables(n_tokens, tile):
    ang_row, ang_col = _rope_tables(n_tokens)
    one = jnp.ones((n_tokens, QK_NOPE), F32)
    zero = jnp.zeros((n_tokens, QK_NOPE), F32)
    pad1 = jnp.ones((n_tokens, QK_PAD - QK_HEAD), F32)
    pad0 = jnp.zeros((n_tokens, QK_PAD - QK_HEAD), F32)
    cr, cc, sr, sc = jnp.cos(ang_row), jnp.cos(ang_col), jnp.sin(ang_row), jnp.sin(ang_col)
    cos = jnp.concatenate([one, cr, cr, cc, cc, pad1], axis=1)
    sin = jnp.concatenate([zero, -sr, sr, -sc, sc, pad0], axis=1)
    cos = jnp.concatenate([cos, jnp.ones((tile, QK_PAD), F32)], axis=0)
    sin = jnp.concatenate([sin, jnp.zeros((tile, QK_PAD), F32)], axis=0)
    return cos, sin


def _rope_tables(n_tokens):
    rows = n_tokens // GRID_W
    row = jnp.broadcast_to(jnp.arange(rows)[:, None], (rows, GRID_W)).reshape(-1).astype(F32)
    col = jnp.broadcast_to(jnp.arange(GRID_W)[None, :], (rows, GRID_W)).reshape(-1).astype(F32)
    inv_freq = ROPE_THETA ** (-jnp.arange(0, ROPE_AXIS, 2, dtype=F32) / ROPE_AXIS)
    return row[:, None] * inv_freq, col[:, None] * inv_freq


def _dft_time_matrix(n):
    f = int(round(math.sqrt(n)))
    assert f * f == n
    k = jnp.arange(n, dtype=jnp.int32)[:, None]
    t = jnp.arange(f, dtype=jnp.int32)[None, :]
    a = ((k * t) % f).astype(F32) * (2.0 * math.pi / f)
    b = ((k * t) % n).astype(F32) * (2.0 * math.pi / n)
    ca, sa = jnp.cos(a)[:, :, None], jnp.sin(a)[:, :, None]
    cb, sb = jnp.cos(b)[:, None, :], jnp.sin(b)[:, None, :]
    cos = (ca * cb - sa * sb).reshape(n, n)
    sin = (sa * cb + ca * sb).reshape(n, n)
    return jnp.concatenate([cos, -sin], axis=1).astype(BF16)


def _dft_channel_matrix():
    idx = jnp.arange(FNO_GROUP_DIM, dtype=jnp.int32)
    ang = ((idx[:, None] * idx[None, :]) % FNO_GROUP_DIM).astype(F32) * (2.0 * math.pi / FNO_GROUP_DIM)
    eye = jnp.eye(FNO_GROUPS, dtype=F32)
    return jnp.concatenate([jnp.kron(eye, jnp.cos(ang)), jnp.kron(eye, jnp.sin(ang))], axis=1)


def _layer(xg, c_act, cctx_act, p, last, consts, dims):
    B, S, Sc = dims
    M, D = xg.shape
    ML = B * S
    dft_x, dft_c, dft_ch, (rope_cos, rope_sin), head_ones = consts

    act = jnp.concatenate([c_act, cctx_act[None, :], jnp.zeros((7 - B, D), F32)], axis=0)
    modtab = matmul_f32_bias(act, p["w_mod"], p["b_mod"])[:B + 1].reshape(B + 1, 6, D)

    w_in = p["w_in"]
    o1 = N_RWKV_IN
    o2 = o1 + N_MLA_IN
    o3 = o2 + FNO_WIDTH
    w_fd = matmul_f32_bias(w_in[:, o2:o3], dft_ch, jnp.zeros((2 * FNO_WIDTH,), F32))
    zr, zm, zg, zd = inproj(xg, modtab, p["g_norm1"], w_in[:, :ZR_PAD].astype(BF16),
                            w_in[:, o1:o1 + ZM_PAD].astype(BF16), w_in[:, o3:].astype(BF16),
                            w_fd.astype(BF16), S)

    r, v, kkf, lw_dir, k_dir, b_dir, g_out, bonus = rwkv_features(zr, p, head_ones, B, S, Sc)
    y_slots = rwkv_scan(r, v, kkf, lw_dir, k_dir, b_dir, B, S, Sc)

    qh, kh, vh = mla_prep(zm, p, rope_cos, rope_sin, ML, S)
    att = attention(qh, kh, vh, B, 0, S, [(0, S), (ML, Sc)])

    yf = time_dft(dft_x, zd, B, 0, 1.0 / math.sqrt(S * FNO_GROUP_DIM))

    n_rows = ML if last else M
    if not last:
        att = jnp.concatenate([att, attention(qh, kh, vh, B, ML, Sc, [(ML, Sc)])], axis=0)
        yf = jnp.concatenate([yf, time_dft(dft_c, zd, B, ML, 1.0 / math.sqrt(Sc * FNO_GROUP_DIM))], axis=0)
    x_mid = post(y_slots, bonus, g_out, p["gn_w"], p["gn_b"], head_ones, att, yf, zg, xg, modtab,
                 p["w_rwkv_o"], p["w_mla_o"], p["w_fno"], p["w_out"], n_rows, S, Sc)
    return ffn(x_mid, modtab, p["g_norm2"], p["w_ffn_in"].astype(BF16), p["w_ffn_out"].astype(BF16), S)


def kernel(x, c, ctx, c_ctx, w_mod, b_mod, g_norm1, w_in, mu_prev, mu_next, w0, w2, a0, a2, k_k, k_a, r_k, g2, gn_w, gn_b, w_rwkv_o, g_cq, g_ckv, w_uq, w_ukv, g_qn, g_kn, w_mla_o, w_fno, w_out, g_norm2, w_ffn_in, w_ffn_out):
    layer_params = dict(w_mod=w_mod, b_mod=b_mod, g_norm1=g_norm1, w_in=w_in, mu_prev=mu_prev,
                        mu_next=mu_next, w0=w0, w2=w2, a0=a0, a2=a2, k_k=k_k, k_a=k_a, r_k=r_k,
                        g2=g2, gn_w=gn_w, gn_b=gn_b, w_rwkv_o=w_rwkv_o, g_cq=g_cq, g_ckv=g_ckv,
                        w_uq=w_uq, w_ukv=w_ukv, g_qn=g_qn, g_kn=g_kn, w_mla_o=w_mla_o, w_fno=w_fno,
                        w_out=w_out, g_norm2=g_norm2, w_ffn_in=w_ffn_in, w_ffn_out=w_ffn_out)
    depth = w_mod.shape[0]
    B, S, D = x.shape
    Sc = ctx.shape[1]
    assert S % ROW_TILE == 0 and (B * Sc) % ROW_TILE == 0 and B <= 7
    head_ones = jnp.kron(jnp.eye(RWKV_HEADS, dtype=F32),
                         jnp.ones((RWKV_HEAD_DIM, RWKV_HEAD_DIM), F32)).astype(BF16)
    consts = (_dft_time_matrix(S), _dft_time_matrix(Sc), _dft_channel_matrix(),
              _rope_lane_tables(S, 256), head_ones)
    c_act = jax.nn.silu(c)
    cctx_act = jax.nn.silu(c_ctx)
    xg = jnp.concatenate([x.reshape(B * S, D), ctx.reshape(B * Sc, D)], axis=0)
    for layer in range(depth):
        p = {name: arr[layer] for name, arr in layer_params.items()}
        xg = _layer(xg, c_act, cctx_act, p, layer == depth - 1, consts, (B, S, Sc))
    return xg[:B * S].reshape(B, S, D)
```

```python
import functools
import math

import jax
import jax.numpy as jnp
from jax import lax
from jax.experimental import pallas as pl
from jax.experimental.pallas import tpu as pltpu

F32 = jnp.float32
BF16 = jnp.bfloat16

D_MODEL = 1024
GRID_W = 64
RWKV_HEADS = 8
RWKV_HEAD_DIM = 64
RWKV_WIDTH = RWKV_HEADS * RWKV_HEAD_DIM
DECAY_LORA = 64
AAA_LORA = 64
GATE_LORA = 128
N_DIR = 2
GN_EPS = 64e-5
MLA_HEADS = 8
QK_NOPE = 64
QK_ROPE = 32
QK_HEAD = QK_NOPE + QK_ROPE
V_HEAD = 64
Q_LORA = 384
KV_LORA = 128
ROPE_AXIS = QK_ROPE // 2
ROPE_THETA = 10000.0
FNO_GROUPS = 8
FNO_GROUP_DIM = 64
FNO_WIDTH = FNO_GROUPS * FNO_GROUP_DIM
N_RWKV_IN = 3 * RWKV_WIDTH + N_DIR * DECAY_LORA + N_DIR * AAA_LORA + GATE_LORA
N_MLA_IN = Q_LORA + KV_LORA + QK_ROPE
RMS_EPS = 1e-6

LANES = 128
CHUNK = 64
QK_PAD = LANES
ROW_TILE = 1024
ZR_PAD = 2048
ZM_PAD = 1024
COL_TILE = 1024
FFN_CHUNK = 512
VMEM_LIMIT = 48 * 1024 * 1024


def _params(*sem):
    return pltpu.CompilerParams(dimension_semantics=sem, vmem_limit_bytes=VMEM_LIMIT)


def _dot(a, b):
    return jnp.dot(a, b, preferred_element_type=F32)


def _dot_nt(a, b):
    return lax.dot_general(a, b, (((1,), (1,)), ((), ())), preferred_element_type=F32)


def _dot_tn(a, b):
    return lax.dot_general(a, b, (((0,), (0,)), ((), ())), preferred_element_type=F32)


def _sigmoid(x):
    return 1.0 / (1.0 + jnp.exp(-x))


def _pick(dim, pref):
    t = min(dim, pref)
    while dim % t:
        t //= 2
    return t


def _mm_f32_body(a_ref, b_ref, bias_ref, o_ref):
    o_ref[...] = jnp.dot(a_ref[...], b_ref[...], preferred_element_type=F32,
                         precision=lax.Precision.HIGHEST) + bias_ref[...]


def matmul_f32_bias(a, b, bias, tn=512):
    M, K = a.shape
    _, N = b.shape
    return pl.pallas_call(
        _mm_f32_body,
        grid=(N // tn,),
        in_specs=[pl.BlockSpec((M, K), lambda j: (0, 0)),
                  pl.BlockSpec((K, tn), lambda j: (0, j)),
                  pl.BlockSpec((1, tn), lambda j: (0, j))],
        out_specs=pl.BlockSpec((M, tn), lambda j: (0, j)),
        out_shape=jax.ShapeDtypeStruct((M, N), F32),
        compiler_params=_params("parallel"),
        name="mm_f32_bias",
    )(a, b, bias.reshape(1, N))


def _norm_mod(x, gain, shift, scale):
    y = x * lax.rsqrt(jnp.mean(x * x, axis=-1, keepdims=True) + RMS_EPS)
    return (y * gain) * (1.0 + scale) + shift


def _inproj_body(x_ref, mod_ref, g_ref, wr_ref, wm_ref, wg_ref, wd_ref, zr_ref, zm_ref, zg_ref, zd_ref,
                 h_ref):
    j = pl.program_id(1)

    @pl.when(j == 0)
    def _():
        h_ref[...] = _norm_mod(x_ref[...], g_ref[...], mod_ref[0:1, :], mod_ref[1:2, :]).astype(BF16)

    @pl.when(j < 2)
    def _():
        zr_ref[...] = _dot(h_ref[...], wr_ref[...])

    @pl.when(j == 2)
    def _():
        zm_ref[...] = _dot(h_ref[...], wm_ref[...])

    @pl.when((j >= 3) & (j < 6))
    def _():
        zg_ref[...] = _sigmoid(_dot(h_ref[...], wg_ref[...])).astype(BF16)

    @pl.when(j == 6)
    def _():
        zd_ref[...] = _dot(h_ref[...], wd_ref[...]).astype(BF16)


def _mod_row(i, tiles_per_sample, n_samples):
    return jnp.minimum(i // tiles_per_sample, n_samples)


def inproj(xg, modtab, gain, w_r, w_m, w_g, w_d, n_lat):
    M, D = xg.shape
    n_samples = modtab.shape[0] - 1
    tps = n_lat // ROW_TILE
    clamp = lambda j, lo, hi: jnp.minimum(jnp.maximum(j, lo), hi)
    zr_col = lambda i, j: clamp(j, 0, 1)
    zg_col = lambda i, j: clamp(j - 3, 0, 2)
    first = lambda i, j: 0
    wtile = lambda col: pl.BlockSpec((D, COL_TILE), lambda i, j: (0, col(i, j)))
    otile = lambda col: pl.BlockSpec((ROW_TILE, COL_TILE), lambda i, j: (i, col(i, j)))
    return pl.pallas_call(
        _inproj_body,
        grid=(M // ROW_TILE, 7),
        in_specs=[pl.BlockSpec((ROW_TILE, D), lambda i, j: (i, 0)),
                  pl.BlockSpec((None, 6, D), lambda i, j: (_mod_row(i, tps, n_samples), 0, 0)),
                  pl.BlockSpec((1, D), lambda i, j: (0, 0)),
                  wtile(zr_col), wtile(first), wtile(zg_col), wtile(first)],
        out_specs=[otile(zr_col), otile(first), otile(zg_col), otile(first)],
        out_shape=[jax.ShapeDtypeStruct((M, ZR_PAD), F32),
                   jax.ShapeDtypeStruct((M, ZM_PAD), F32),
                   jax.ShapeDtypeStruct((M, 3 * D), BF16),
                   jax.ShapeDtypeStruct((M, 2 * FNO_WIDTH), BF16)],
        scratch_shapes=[pltpu.VMEM((ROW_TILE, D), BF16)],
        compiler_params=_params("parallel", "arbitrary"),
        name="inproj",
    )(xg, modtab, gain.reshape(1, D), w_r, w_m, w_g, w_d)


def _head_sum(t, hs, terms=1):
    parts = _split3(t)[:terms]
    return sum(_dot(part, hs) for part in parts)


def _slot_tile(i, tm, n_lat, n_ctx, n_samples, nb):
    ng = n_samples // nb
    tps, lat_tiles = n_lat // tm, n_samples * n_lat // tm
    ctx_tiles = ng * n_ctx // tm
    sample = i // tps
    c = i - lat_tiles
    slot = jnp.where(i < lat_tiles, sample // ng, c // ctx_tiles)
    local = jnp.where(i < lat_tiles, (sample % ng) * tps + i % tps, ng * tps + c % ctx_tiles)
    return slot, local


def _post_body(*refs, nb, slot_of):
    y_refs = refs[:2 * nb]
    (bonus_ref, gate_ref, gnw_ref, gnb_ref, hs_ref, att_ref, four_ref, zg_ref, x_ref, mod_ref,
     wr_ref, wm_ref, wf_ref, wo_ref, out_ref) = refs[2 * nb:]
    D = x_ref.shape[1]
    slot = slot_of(pl.program_id(0))
    y = y_refs[0][...] + y_refs[nb][...]
    for j in range(1, nb):
        y = jnp.where(slot == j, y_refs[j][...] + y_refs[nb + j][...], y)
    hs = hs_ref[...]
    inv_n = 1.0 / RWKV_HEAD_DIM
    dev = y - _head_sum(y, hs, terms=2) * inv_n
    var = _head_sum(dev * dev, hs) * inv_n
    yn = dev * lax.rsqrt(var + GN_EPS) * gnw_ref[...] + gnb_ref[...]
    o = ((yn + bonus_ref[...]) * gate_ref[...]).astype(BF16)
    a = _dot(o, wr_ref[...])
    b = _dot(att_ref[...], wm_ref[...])
    f = _dot(four_ref[...], wf_ref[...])
    merged = (zg_ref[:, 0:D].astype(F32) * a + zg_ref[:, D:2 * D].astype(F32) * b
              + zg_ref[:, 2 * D:3 * D].astype(F32) * f)
    out_ref[...] = x_ref[...] + mod_ref[2:3, :] * _dot(merged.astype(BF16), wo_ref[...])


def post(y_slots, bonus, gate, gn_w, gn_b, hs, att, four, zg, xg, modtab, w_r, w_m, w_f, w_o,
         n_rows, n_lat, n_ctx, tm=512):
    D = xg.shape[1]
    W = bonus.shape[1]
    n_samples = modtab.shape[0] - 1
    nb = len(y_slots) // N_DIR
    assert (n_samples // nb * n_ctx) % tm == 0 and n_lat % tm == 0
    tps = n_lat // tm
    slot_tile = functools.partial(_slot_tile, tm=tm, n_lat=n_lat, n_ctx=n_ctx, n_samples=n_samples, nb=nb)
    row = lambda i: (i, 0)
    whole = lambda i: (0, 0)
    rows = lambda width: pl.BlockSpec((tm, width), row)
    wspec = pl.BlockSpec((W, D), whole)
    vec = pl.BlockSpec((1, W), whole)

    def slot_spec(j):
        def index(i):
            slot, local = slot_tile(i)
            return (jnp.where(slot == j, local, 0), 0)
        return pl.BlockSpec((tm, W), index)

    return pl.pallas_call(
        functools.partial(_post_body, nb=nb, slot_of=lambda i: slot_tile(i)[0]),
        grid=(n_rows // tm,),
        in_specs=[slot_spec(j) for _ in range(N_DIR) for j in range(nb)]
                 + [rows(W), rows(W), vec, vec, pl.BlockSpec((W, W), whole),
                    rows(att.shape[1]), rows(four.shape[1]), rows(3 * D), rows(D),
                    pl.BlockSpec((None, 6, D), lambda i: (_mod_row(i, tps, n_samples), 0, 0)),
                    wspec, wspec, wspec, pl.BlockSpec((D, D), whole)],
        out_specs=rows(D),
        out_shape=jax.ShapeDtypeStruct((n_rows, D), F32),
        compiler_params=_params("parallel"),
        name="post",
    )(*y_slots, bonus, gate, gn_w.reshape(1, W), gn_b.reshape(1, W), hs, att, four, zg, xg, modtab,
      w_r.astype(BF16), w_m.astype(BF16), w_f.astype(BF16), w_o.astype(BF16))


def _feat_body(z_ref, zp_ref, zn_ref, mup_ref, mun_ref, w2_ref, a2_ref, g2_ref, w0_ref, a0_ref, kk_ref,
               ka_ref, rk_ref, hs_ref, r_ref, v_ref, kkn_ref, lw_ref, kd_ref, bd_ref, gate_ref, bonus_ref,
               *, n_lat_rows, seg_lat, seg_ctx):
    tm = z_ref.shape[0]
    nz = N_RWKV_IN
    W = RWKV_WIDTH
    i = pl.program_id(0)
    z = z_ref[:, :nz]
    loc = lax.broadcasted_iota(jnp.int32, (tm, 1), 0)
    row = loc + i * tm
    is_lat = row < n_lat_rows
    pos = jnp.where(is_lat, row & (seg_lat - 1), (row - n_lat_rows) & (seg_ctx - 1))
    seg_last = jnp.where(is_lat, seg_lat - 1, seg_ctx - 1)
    z_prev = jnp.where(loc == 0, zp_ref[7:8, :nz], pltpu.roll(z, 1, 0))
    z_prev = jnp.where(pos == 0, 0.0, z_prev)
    z_next = jnp.where(loc == tm - 1, zn_ref[0:1, :nz], pltpu.roll(z, tm - 1, 0))
    z_next = jnp.where(pos == seg_last, 0.0, z_next)
    z = z + (z_prev - z) * mup_ref[...] + (z_next - z) * mun_ref[...]

    r, k, v = z[:, :W], z[:, W:2 * W], z[:, 2 * W:3 * W]
    zw = jnp.tanh(z[:, 3 * W:3 * W + LANES]).astype(BF16)
    za = z[:, 3 * W + LANES:3 * W + 2 * LANES].astype(BF16)
    zg = _sigmoid(z[:, 3 * W + 2 * LANES:3 * W + 3 * LANES]).astype(BF16)
    lw = (-math.exp(-0.5)) * _sigmoid(_dot(zw, w2_ref[...]) + w0_ref[...])
    a = _sigmoid(_dot(za, a2_ref[...]) + a0_ref[...])
    hs = hs_ref[...]
    kq = k * kk_ref[...]
    kkn = kq * lax.rsqrt(jnp.maximum(_head_sum(kq * kq, hs), 1e-24))
    k_sum = None
    for d in range(N_DIR):
        a_d = a[:, d * W:(d + 1) * W]
        k_d = k * (1.0 + (a_d - 1.0) * ka_ref[...])
        lw_ref[d] = lw[:, d * W:(d + 1) * W]
        kd_ref[d] = k_d
        bd_ref[d] = kkn * a_d
        k_sum = k_d if k_sum is None else k_sum + k_d
    r_ref[...] = r
    v_ref[...] = v
    kkn_ref[...] = kkn
    gate_ref[...] = _dot(zg, g2_ref[...])
    bonus_ref[...] = _head_sum(r * rk_ref[...] * k_sum, hs) * v


def rwkv_features(zr, p, hs, n_samples, n_lat, n_ctx, tm=256):
    M, ZW = zr.shape
    W = RWKV_WIDTH
    assert n_lat & (n_lat - 1) == 0 and n_ctx & (n_ctx - 1) == 0 and n_lat % tm == 0 and n_ctx % tm == 0
    pad = lambda t: t.reshape(1, N_RWKV_IN)
    zero = jnp.zeros((DECAY_LORA, W), F32)
    blockdiag = lambda t: jnp.concatenate(
        [jnp.concatenate([t[0], zero], axis=1), jnp.concatenate([zero, t[1]], axis=1)], axis=0).astype(BF16)
    row = lambda i: (i, 0)
    whole = lambda i: (0, 0)
    halo = tm // 8
    full = lambda t: pl.BlockSpec(t.shape, whole)
    params = [pad(p["mu_prev"]), pad(p["mu_next"]), blockdiag(p["w2"]), blockdiag(p["a2"]),
              p["g2"].astype(BF16), p["w0"].reshape(1, N_DIR * W), p["a0"].reshape(1, N_DIR * W),
              p["k_k"].reshape(1, W), p["k_a"].reshape(1, W), p["r_k"].reshape(1, W), hs]
    one = pl.BlockSpec((tm, W), row)
    two = pl.BlockSpec((N_DIR, tm, W), lambda i: (0, i, 0))
    s1 = jax.ShapeDtypeStruct((M, W), F32)
    s2 = jax.ShapeDtypeStruct((N_DIR, M, W), F32)
    return pl.pallas_call(
        functools.partial(_feat_body, n_lat_rows=n_samples * n_lat, seg_lat=n_lat, seg_ctx=n_ctx),
        grid=(M // tm,),
        in_specs=[pl.BlockSpec((tm, ZW), row),
                  pl.BlockSpec((8, ZW), lambda i: (jnp.maximum(i * halo - 1, 0), 0)),
                  pl.BlockSpec((8, ZW), lambda i: (jnp.minimum((i + 1) * halo, M // 8 - 1), 0))]
                 + [full(t) for t in params],
        out_specs=[one, one, one, two, two, two, one, one],
        out_shape=[s1, s1, s1, s2, s2, s2, s1, s1],
        compiler_params=_params("parallel"),
        name="rwkv_features",
    )(zr, zr, zr, *params)


def _ffn_body(x_ref, mod_ref, g_ref, wg_ref, wu_ref, wo_ref, out_ref, h_ref, acc_ref, *, n_chunks):
    c = pl.program_id(1)

    @pl.when(c == 0)
    def _():
        h_ref[...] = _norm_mod(x_ref[...], g_ref[...], mod_ref[3:4, :], mod_ref[4:5, :]).astype(BF16)
        acc_ref[...] = jnp.zeros_like(acc_ref)

    h = h_ref[...]
    gate = _dot(h, wg_ref[...])
    up = _dot(h, wu_ref[...])
    act = gate * _sigmoid(gate) * up
    acc_ref[...] += _dot(act.astype(BF16), wo_ref[...])

    @pl.when(c == n_chunks - 1)
    def _():
        out_ref[...] = x_ref[...] + mod_ref[5:6, :] * acc_ref[...]


def ffn(xg, modtab, gain, w_in, w_out, n_lat):
    M, D = xg.shape
    hidden = w_out.shape[0]
    n_samples = modtab.shape[0] - 1
    tps = n_lat // ROW_TILE
    n_chunks = pl.cdiv(hidden, FFN_CHUNK)
    padded = n_chunks * FFN_CHUNK
    w_in = jnp.pad(w_in.reshape(D, 2, hidden), ((0, 0), (0, 0), (0, padded - hidden)))
    w_in = w_in.reshape(D, 2 * padded).astype(BF16)
    w_out = jnp.pad(w_out, ((0, padded - hidden), (0, 0))).astype(BF16)
    return pl.pallas_call(
        functools.partial(_ffn_body, n_chunks=n_chunks),
        grid=(M // ROW_TILE, n_chunks),
        in_specs=[pl.BlockSpec((ROW_TILE, D), lambda i, c: (i, 0)),
                  pl.BlockSpec((None, 6, D), lambda i, c: (_mod_row(i, tps, n_samples), 0, 0)),
                  pl.BlockSpec((1, D), lambda i, c: (0, 0)),
                  pl.BlockSpec((D, FFN_CHUNK), lambda i, c: (0, c)),
                  pl.BlockSpec((D, FFN_CHUNK), lambda i, c: (0, n_chunks + c)),
                  pl.BlockSpec((FFN_CHUNK, D), lambda i, c: (c, 0))],
        out_specs=pl.BlockSpec((ROW_TILE, D), lambda i, c: (i, 0)),
        out_shape=jax.ShapeDtypeStruct((M, D), F32),
        scratch_shapes=[pltpu.VMEM((ROW_TILE, D), BF16), pltpu.VMEM((ROW_TILE, D), F32)],
        compiler_params=_params("parallel", "arbitrary"),
        name="ffn",
    )(xg, modtab, gain.reshape(1, D), w_in, w_in, w_out)


def _dft_body(w_ref, z_ref, o_ref, acc_ref, *, nk, scale):
    k = pl.program_id(2)

    @pl.when(k == 0)
    def _():
        acc_ref[...] = jnp.zeros_like(acc_ref)

    acc_ref[...] += _dot(w_ref[...], z_ref[...])

    @pl.when(k == nk - 1)
    def _():
        o_ref[...] = (acc_ref[...] * scale).astype(o_ref.dtype)


def time_dft(w, z, n_samples, row0, scale, tm=1024, tk=2048):
    n = w.shape[0]
    Wd = z.shape[1] // 2
    tm, tk = _pick(n, tm), _pick(n, tk)
    nkh = n // tk
    nk = 2 * nkh
    assert row0 % tk == 0
    off = row0 // tk
    return pl.pallas_call(
        functools.partial(_dft_body, nk=nk, scale=scale),
        grid=(n_samples, n // tm, nk),
        in_specs=[pl.BlockSpec((tm, tk), lambda b, i, k: (i, k)),
                  pl.BlockSpec((tk, Wd), lambda b, i, k: (off + b * nkh + k % nkh, k // nkh))],
        out_specs=pl.BlockSpec((tm, Wd), lambda b, i, k: (b * (n // tm) + i, 0)),
        out_shape=jax.ShapeDtypeStruct((n_samples * n, Wd), BF16),
        scratch_shapes=[pltpu.VMEM((tm, Wd), F32)],
        compiler_params=_params("parallel", "parallel", "arbitrary"),
        name="time_dft",
    )(w, z)


def _delta_chains(chains, m_a, bdmask):
    C = CHUNK
    bf = lambda t: t.astype(BF16)
    zero = jnp.zeros((), BF16)
    cat = lambda ts: jnp.concatenate(ts, axis=0)

    def blk(t):
        t = bf(t)
        return cat([jnp.where(m_a, t, zero), jnp.where(m_a, zero, t)])

    a4 = [_dot_nt(cat([bf(c["ah"]), bf(c["rh"])]), cat([blk(c["bh"]), blk(c["kh"])])) for c in chains]
    a_ab = [jnp.where(c["strict2"], a[:C, :2 * C], 0.0) for c, a in zip(chains, a4)]
    a_ak = [jnp.where(c["strict2"], a[:C, 2 * C:], 0.0) for c, a in zip(chains, a4)]
    a_r = [jnp.where(c["incl4"], a[C:], 0.0) for c, a in zip(chains, a4)]
    x = a_ab
    pw = a_ab
    pwb = [blk(t) for t in pw]
    for _ in range(int(math.log2(C)) - 1):
        pw = [_dot(bf(t), tb) for t, tb in zip(pw, pwb)]
        pwb = [blk(t) for t in pw]
        x = [xi + t + _dot(bf(xi), tb) for xi, t, tb in zip(x, pw, pwb)]
    spb = [bf(c["sp"]) for c in chains]
    vb = [blk(c["v"]) for c in chains]
    x1 = [_dot_nt(bf(c["ah"]), s) + _dot(bf(a), v) for c, s, a, v in zip(chains, spb, a_ak, vb)]
    u = [t + _dot(bf(xi), blk(t)) for t, xi in zip(x1, x)]
    y = [_dot_nt(bf(c["rh"]), s) + _dot(bf(a), cat([blk(ui), v]))
         for c, s, a, ui, v in zip(chains, spb, a_r, u, vb)]
    upd = [_dot_tn(cat([bf(ui), bf(c["v"])]), cat([bf(c["bb"]), bf(c["kb"])])) for c, ui in zip(chains, u)]
    sp_new = [c["sp"] * c["dtot"] + jnp.where(bdmask, t, 0.0) for c, t in zip(chains, upd)]
    return list(zip(y, sp_new))


def _split3(t):
    hi = t.astype(BF16)
    r1 = t - hi.astype(F32)
    mid = r1.astype(BF16)
    lo = (r1 - mid.astype(F32)).astype(BF16)
    return hi, mid, lo


def _delta_prep(r, lw, k, v, kk, b, rev):
    C = CHUNK

    def causal(width, strict):
        ti = lax.broadcasted_iota(jnp.int32, (C, width), 0)
        ii = lax.broadcasted_iota(jnp.int32, (C, width), 1) & (C - 1)
        if rev:
            return ii > ti if strict else ii >= ti
        return ii < ti if strict else ii <= ti

    tri = causal(C, False).astype(BF16)
    L = sum(_dot(tri, part) for part in _split3(lw))
    last = 0 if rev else C - 1
    ltot = L[last:last + 1, :]
    e_nl = jnp.exp(-L)
    e_t = jnp.exp(ltot - L)
    return dict(ah=-kk * jnp.exp(L - lw), rh=r * jnp.exp(L), bh=b * e_nl, kh=k * e_nl, bb=b * e_t,
                kb=k * e_t, v=v, dtot=jnp.exp(ltot)), causal(2 * C, True), causal(4 * C, False)


def _delta_body(*refs, nb):
    ins, outs, s_ref = refs[:12 * nb], refs[12 * nb:14 * nb], refs[14 * nb]

    @pl.when(pl.program_id(1) == 0)
    def _():
        s_ref[...] = jnp.zeros_like(s_ref)

    C, W = ins[0].shape
    m_a = lax.broadcasted_iota(jnp.int32, (C, LANES), 1) < (LANES // 2)
    br = lax.broadcasted_iota(jnp.int32, (LANES, LANES), 0) < (LANES // 2)
    bc = lax.broadcasted_iota(jnp.int32, (LANES, LANES), 1) < (LANES // 2)
    bdmask = br == bc
    chains, dest = [], []
    for d in range(N_DIR):
        for j in range(nb):
            r_ref, v_ref, kk_ref, lw_ref, k_ref, b_ref = ins[(d * nb + j) * 6:(d * nb + j + 1) * 6]
            ops, strict2, incl4 = _delta_prep(r_ref[...], lw_ref[...], k_ref[...], v_ref[...], kk_ref[...],
                                              b_ref[...], rev=(d == 1))
            for p in range(W // LANES):
                sl = slice(p * LANES, (p + 1) * LANES)
                c = {name: t[:, sl] for name, t in ops.items()}
                c.update(sp=s_ref[d, j, p], strict2=strict2, incl4=incl4)
                chains.append(c)
                dest.append((outs[d * nb + j], d, j, p, sl))
    for (y, sp_new), (y_ref, d, j, p, sl) in zip(_delta_chains(chains, m_a, bdmask), dest):
        y_ref[:, sl] = y
        s_ref[d, j, p] = sp_new


def rwkv_scan(r, v, kk, lw, k, b, n_samples, n_lat, n_ctx, samples=2):
    M, W = r.shape
    ncl, ncc = n_lat // CHUNK, n_ctx // CHUNK
    nc = ncl + ncc
    nb = _pick(n_samples, samples)
    ng = n_samples // nb

    def row_block(n_s, s, t):
        return jnp.where(t < ncc, n_s * ncl + s * ncc + t, s * ncl + (t - ncc))

    def step_chunk(d, c):
        if d == 0:
            return c
        return jnp.where(c < ncc, ncc - 1 - c, nc + ncc - 1 - c)

    in_specs, out_specs, args = [], [], []
    for d in range(N_DIR):
        for j in range(nb):
            def rows(bi, c, d=d, j=j):
                return row_block(n_samples, j * ng + bi, step_chunk(d, c))

            def out_rows(bi, c, d=d):
                return row_block(ng, bi, step_chunk(d, c))
            shared = pl.BlockSpec((CHUNK, W), lambda bi, c, rows=rows: (rows(bi, c), 0))
            per_dir = pl.BlockSpec((None, CHUNK, W), lambda bi, c, rows=rows, d=d: (d, rows(bi, c), 0))
            in_specs += [shared] * 3 + [per_dir] * 3
            args += [r, v, kk, lw, k, b]
            out_specs.append(pl.BlockSpec((CHUNK, W), lambda bi, c, out_rows=out_rows: (out_rows(bi, c), 0)))
    outs = pl.pallas_call(
        functools.partial(_delta_body, nb=nb),
        grid=(ng, nc),
        in_specs=in_specs,
        out_specs=out_specs,
        out_shape=[jax.ShapeDtypeStruct((M // nb, W), F32)] * (N_DIR * nb),
        scratch_shapes=[pltpu.VMEM((N_DIR, nb, W // LANES, LANES, LANES), F32)],
        compiler_params=_params("parallel", "arbitrary"),
        name="delta_scan",
    )(*args)
    return outs


VT_ROWS = V_HEAD + 16
ATTN_LOOKAHEAD = 6


def _attn_body(*refs, n_seg, kv_chunk):
    q_ref, kv, o_ref = refs[0], refs[1:1 + 2 * n_seg], refs[1 + 2 * n_seg]
    items = []
    for g in range(n_seg):
        n = kv[2 * g].shape[0]
        ck = min(kv_chunk, n)
        items += [(hh, g, c0, ck) for c0 in range(0, n, ck) for hh in range(2)]
    q = [q_ref[:, hh * QK_PAD:(hh + 1) * QK_PAD] for hh in range(2)]
    m = [None, None]
    acc = [None, None]
    scores = {}
    for idx in range(len(items) + ATTN_LOOKAHEAD):
        if idx < len(items):
            hh, g, c0, ck = items[idx]
            scores[idx] = _dot_nt(kv[2 * g][c0:c0 + ck, hh * QK_PAD:(hh + 1) * QK_PAD], q[hh])
        if idx >= ATTN_LOOKAHEAD:
            hh, g, c0, ck = items[idx - ATTN_LOOKAHEAD]
            st = scores.pop(idx - ATTN_LOOKAHEAD)
            mc = jnp.max(st, axis=0, keepdims=True)
            m_new = mc if m[hh] is None else jnp.maximum(m[hh], mc)
            pt = jnp.exp2(st - m_new).astype(BF16)
            pv = _dot(kv[2 * g + 1][hh * QK_PAD:hh * QK_PAD + VT_ROWS, c0:c0 + ck], pt)
            acc[hh] = pv if m[hh] is None else acc[hh] * jnp.exp2(m[hh] - m_new) + pv
            m[hh] = m_new
    outs = [a[:V_HEAD] / a[V_HEAD:V_HEAD + 1] for a in acc]
    o_ref[...] = jnp.concatenate(outs, axis=0).T.astype(o_ref.dtype)


def attention(q, k, vt, n_samples, q_row0, n_q, segments, tq=256, kv_chunk=256):
    H = q.shape[1] // QK_PAD
    tq = _pick(n_q, tq)
    assert q_row0 % tq == 0
    nq = n_q // tq
    in_specs = [pl.BlockSpec((tq, 2 * QK_PAD), lambda b, h, i: (q_row0 // tq + b * nq + i, h))]
    args = [q]
    for row0, n in segments:
        assert row0 % n == 0
        in_specs += [pl.BlockSpec((n, 2 * QK_PAD), lambda b, h, i, row0=row0, n=n: (row0 // n + b, h)),
                     pl.BlockSpec((2 * QK_PAD, n), lambda b, h, i, row0=row0, n=n: (h, row0 // n + b))]
        args += [k, vt]
    return pl.pallas_call(
        functools.partial(_attn_body, n_seg=len(segments), kv_chunk=kv_chunk),
        grid=(n_samples, H // 2, nq),
        in_specs=in_specs,
        out_specs=pl.BlockSpec((tq, 2 * V_HEAD), lambda b, h, i: (b * nq + i, h)),
        out_shape=jax.ShapeDtypeStruct((n_samples * n_q, H * V_HEAD), BF16),
        compiler_params=_params("parallel", "parallel", "parallel"),
        name="attention",
    )(*args)


def _lane_sum(t, ones):
    tb = t.astype(BF16)
    return jnp.concatenate([_dot(tb[:, c:c + LANES], ones) for c in range(0, t.shape[1], LANES)], axis=1)


def _mla_body(z_ref, gcq_ref, gckv_ref, wq_ref, wk_ref, wv_ref, gq_ref, gk_ref, cos_ref, sin_ref, ones_ref,
              q_ref, k_ref, vt_ref, *, q_scale):
    tm = z_ref.shape[0]
    H = MLA_HEADS
    ones = ones_ref[...]
    c_q = z_ref[:, :Q_LORA]
    c_kv = z_ref[:, Q_LORA:Q_LORA + KV_LORA]
    cq_n = c_q * lax.rsqrt(jnp.mean(c_q * c_q, axis=-1, keepdims=True) + RMS_EPS) * gcq_ref[...]
    ckv_n = (c_kv * lax.rsqrt(jnp.mean(c_kv * c_kv, axis=-1, keepdims=True) + RMS_EPS)
             * gckv_ref[...]).astype(BF16)
    lane = lax.broadcasted_iota(jnp.int32, (tm, LANES), 1)
    k_rope = pltpu.roll(jnp.where(lane < QK_ROPE, z_ref[:, Q_LORA + KV_LORA:Q_LORA + KV_LORA + LANES], 0.0),
                        QK_NOPE, 1)
    q = _dot(cq_n.astype(BF16), wq_ref[...])
    k = _dot(ckv_n, wk_ref[...]) + jnp.concatenate([k_rope] * H, axis=1)
    cos = jnp.concatenate([cos_ref[...]] * H, axis=1)
    sin = jnp.concatenate([sin_ref[...]] * H, axis=1)
    first_half = (lax.broadcasted_iota(jnp.int32, (tm, H * LANES), 1) & (ROPE_AXIS // 2)) == 0

    def finish(t, gain, scale):
        t = t * lax.rsqrt(_lane_sum(t * t, ones) * (1.0 / QK_HEAD) + RMS_EPS) * gain
        partner = jnp.where(first_half, pltpu.roll(t, H * LANES - ROPE_AXIS // 2, 1),
                            pltpu.roll(t, ROPE_AXIS // 2, 1))
        return ((t * cos + partner * sin) * scale).astype(BF16)

    q_ref[...] = finish(q, gq_ref[...], q_scale)
    k_ref[...] = finish(k, gk_ref[...], 1.0)
    v = _dot(ckv_n, wv_ref[...])
    ones_rows = (lax.broadcasted_iota(jnp.int32, (tm, H * LANES), 1) & (LANES - 1)) >= V_HEAD
    vt_ref[...] = jnp.where(ones_rows, 1.0, v).T.astype(BF16)


def mla_prep(zm, p, rope_cos, rope_sin, n_lat_rows, n_lat, tm=256):
    M, ZW = zm.shape
    H = MLA_HEADS
    hpad = lambda t, n: jnp.pad(t, ((0, 0), (0, 0), (0, QK_PAD - n))).reshape(t.shape[0], H * QK_PAD)
    w_uq = hpad(p["w_uq"].reshape(Q_LORA, H, QK_HEAD), QK_HEAD).astype(BF16)
    w_ukv = p["w_ukv"].reshape(KV_LORA, H, QK_NOPE + V_HEAD)
    w_uk = hpad(w_ukv[..., :QK_NOPE], QK_NOPE).astype(BF16)
    w_uv = hpad(w_ukv[..., QK_NOPE:], V_HEAD).astype(BF16)
    gain = lambda g: jnp.tile(jnp.pad(g, (0, QK_PAD - QK_HEAD)), H).reshape(1, H * QK_PAD)
    ones = jnp.ones((LANES, LANES), BF16)
    consts = [p["g_cq"].reshape(1, Q_LORA), p["g_ckv"].reshape(1, KV_LORA), w_uq, w_uk, w_uv,
              gain(p["g_qn"]), gain(p["g_kn"])]
    row = lambda i: (i, 0)
    whole = lambda i: (0, 0)
    full = lambda t: pl.BlockSpec(t.shape, whole)
    lat_tiles, per_sample = n_lat_rows // tm, n_lat // tm
    tab = lambda i: (jnp.where(i < lat_tiles, i % per_sample, per_sample), 0)
    return pl.pallas_call(
        functools.partial(_mla_body, q_scale=QK_HEAD ** -0.5 * math.log2(math.e)),
        grid=(M // tm,),
        in_specs=[pl.BlockSpec((tm, ZW), row)] + [full(t) for t in consts]
                 + [pl.BlockSpec((tm, LANES), tab), pl.BlockSpec((tm, LANES), tab), full(ones)],
        out_specs=[pl.BlockSpec((tm, H * QK_PAD), row), pl.BlockSpec((tm, H * QK_PAD), row),
                   pl.BlockSpec((H * QK_PAD, tm), lambda i: (0, i))],
        out_shape=[jax.ShapeDtypeStruct((M, H * QK_PAD), BF16), jax.ShapeDtypeStruct((M, H * QK_PAD), BF16),
                   jax.ShapeDtypeStruct((H * QK_PAD, M), BF16)],
        compiler_params=_params("parallel"),
        name="mla_prep",
    )(zm, *consts, rope_cos, rope_sin, ones)


def _rope_lane_tables(n_tokens, tile):
    ang_row, ang_col = _rope_tables(n_tokens)
    one = jnp.ones((n_tokens, QK_NOPE), F32)
    zero = jnp.zeros((n_tokens, QK_NOPE), F32)
    pad1 = jnp.ones((n_tokens, QK_PAD - QK_HEAD), F32)
    pad0 = jnp.zeros((n_tokens, QK_PAD - QK_HEAD), F32)
    cr, cc, sr, sc = jnp.cos(ang_row), jnp.cos(ang_col), jnp.sin(ang_row), jnp.sin(ang_col)
    cos = jnp.concatenate([one, cr, cr, cc, cc, pad1], axis=1)
    sin = jnp.concatenate([zero, -sr, sr, -sc, sc, pad0], axis=1)
    cos = jnp.concatenate([cos, jnp.ones((tile, QK_PAD), F32)], axis=0)
    sin = jnp.concatenate([sin, jnp.zeros((tile, QK_PAD), F32)], axis=0)
    return cos, sin


def _rope_tables(n_tokens):
    rows = n_tokens // GRID_W
    row = jnp.broadcast_to(jnp.arange(rows)[:, None], (rows, GRID_W)).reshape(-1).astype(F32)
    col = jnp.broadcast_to(jnp.arange(GRID_W)[None, :], (rows, GRID_W)).reshape(-1).astype(F32)
    inv_freq = ROPE_THETA ** (-jnp.arange(0, ROPE_AXIS, 2, dtype=F32) / ROPE_AXIS)
    return row[:, None] * inv_freq, col[:, None] * inv_freq


def _dft_time_matrix(n):
    f = int(round(math.sqrt(n)))
    assert f * f == n
    k = jnp.arange(n, dtype=jnp.int32)[:, None]
    t = jnp.arange(f, dtype=jnp.int32)[None, :]
    a = ((k * t) % f).astype(F32) * (2.0 * math.pi / f)
    b = ((k * t) % n).astype(F32) * (2.0 * math.pi / n)
    ca, sa = jnp.cos(a)[:, :, None], jnp.sin(a)[:, :, None]
    cb, sb = jnp.cos(b)[:, None, :], jnp.sin(b)[:, None, :]
    cos = (ca * cb - sa * sb).reshape(n, n)
    sin = (sa * cb + ca * sb).reshape(n, n)
    return jnp.concatenate([cos, -sin], axis=1).astype(BF16)


def _dft_channel_matrix():
    idx = jnp.arange(FNO_GROUP_DIM, dtype=jnp.int32)
    ang = ((idx[:, None] * idx[None, :]) % FNO_GROUP_DIM).astype(F32) * (2.0 * math.pi / FNO_GROUP_DIM)
    eye = jnp.eye(FNO_GROUPS, dtype=F32)
    return jnp.concatenate([jnp.kron(eye, jnp.cos(ang)), jnp.kron(eye, jnp.sin(ang))], axis=1)


def _layer(xg, c_act, cctx_act, p, last, consts, dims):
    B, S, Sc = dims
    M, D = xg.shape
    ML = B * S
    dft_x, dft_c, dft_ch, (rope_cos, rope_sin), head_ones = consts

    act = jnp.concatenate([c_act, cctx_act[None, :], jnp.zeros((7 - B, D), F32)], axis=0)
    modtab = matmul_f32_bias(act, p["w_mod"], p["b_mod"])[:B + 1].reshape(B + 1, 6, D)

    w_in = p["w_in"]
    o1 = N_RWKV_IN
    o2 = o1 + N_MLA_IN
    o3 = o2 + FNO_WIDTH
    w_fd = matmul_f32_bias(w_in[:, o2:o3], dft_ch, jnp.zeros((2 * FNO_WIDTH,), F32))
    zr, zm, zg, zd = inproj(xg, modtab, p["g_norm1"], w_in[:, :ZR_PAD].astype(BF16),
                            w_in[:, o1:o1 + ZM_PAD].astype(BF16), w_in[:, o3:].astype(BF16),
                            w_fd.astype(BF16), S)

    r, v, kkf, lw_dir, k_dir, b_dir, g_out, bonus = rwkv_features(zr, p, head_ones, B, S, Sc)
    y_slots = rwkv_scan(r, v, kkf, lw_dir, k_dir, b_dir, B, S, Sc)

    qh, kh, vh = mla_prep(zm, p, rope_cos, rope_sin, ML, S)
    att = attention(qh, kh, vh, B, 0, S, [(0, S), (ML, Sc)])

    yf = time_dft(dft_x, zd, B, 0, 1.0 / math.sqrt(S * FNO_GROUP_DIM))

    n_rows = ML if last else M
    if not last:
        att = jnp.concatenate([att, attention(qh, kh, vh, B, ML, Sc, [(ML, Sc)])], axis=0)
        yf = jnp.concatenate([yf, time_dft(dft_c, zd, B, ML, 1.0 / math.sqrt(Sc * FNO_GROUP_DIM))], axis=0)
    x_mid = post(y_slots, bonus, g_out, p["gn_w"], p["gn_b"], head_ones, att, yf, zg, xg, modtab,
                 p["w_rwkv_o"], p["w_mla_o"], p["w_fno"], p["w_out"], n_rows, S, Sc)
    return ffn(x_mid, modtab, p["g_norm2"], p["w_ffn_in"], p["w_ffn_out"], S)


def kernel(x, c, ctx, c_ctx, w_mod, b_mod, g_norm1, w_in, mu_prev, mu_next, w0, w2, a0, a2, k_k, k_a, r_k, g2, gn_w, gn_b, w_rwkv_o, g_cq, g_ckv, w_uq, w_ukv, g_qn, g_kn, w_mla_o, w_fno, w_out, g_norm2, w_ffn_in, w_ffn_out):
    layer_params = dict(w_mod=w_mod, b_mod=b_mod, g_norm1=g_norm1, w_in=w_in, mu_prev=mu_prev,
                        mu_next=mu_next, w0=w0, w2=w2, a0=a0, a2=a2, k_k=k_k, k_a=k_a, r_k=r_k,
                        g2=g2, gn_w=gn_w, gn_b=gn_b, w_rwkv_o=w_rwkv_o, g_cq=g_cq, g_ckv=g_ckv,
                        w_uq=w_uq, w_ukv=w_ukv, g_qn=g_qn, g_kn=g_kn, w_mla_o=w_mla_o, w_fno=w_fno,
                        w_out=w_out, g_norm2=g_norm2, w_ffn_in=w_ffn_in, w_ffn_out=w_ffn_out)
    depth = w_mod.shape[0]
    B, S, D = x.shape
    Sc = ctx.shape[1]
    assert S % ROW_TILE == 0 and (B * Sc) % ROW_TILE == 0 and B <= 7
    head_ones = jnp.kron(jnp.eye(RWKV_HEADS, dtype=F32),
                         jnp.ones((RWKV_HEAD_DIM, RWKV_HEAD_DIM), F32)).astype(BF16)
    consts = (_dft_time_matrix(S), _dft_time_matrix(Sc), _dft_channel_matrix(),
              _rope_lane_tables(S, 256), head_ones)
    c_act = jax.nn.silu(c)
    cctx_act = jax.nn.silu(c_ctx)
    xg = jnp.concatenate([x.reshape(B * S, D), ctx.reshape(B * Sc, D)], axis=0)
    for layer in range(depth):
        p = {name: arr[layer] for name, arr in layer_params.items()}
        xg = _layer(xg, c_act, cctx_act, p, layer == depth - 1, consts, (B, S, Sc))
    return xg[:B * S].reshape(B, S, D)
```

```python
import functools
import math

import jax
import jax.numpy as jnp
from jax import lax
from jax.experimental import pallas as pl
from jax.experimental.pallas import tpu as pltpu

F32 = jnp.float32
BF16 = jnp.bfloat16

D_MODEL = 1024
GRID_W = 64
RWKV_HEADS = 8
RWKV_HEAD_DIM = 64
RWKV_WIDTH = RWKV_HEADS * RWKV_HEAD_DIM
DECAY_LORA = 64
AAA_LORA = 64
GATE_LORA = 128
N_DIR = 2
GN_EPS = 64e-5
MLA_HEADS = 8
QK_NOPE = 64
QK_ROPE = 32
QK_HEAD = QK_NOPE + QK_ROPE
V_HEAD = 64
Q_LORA = 384
KV_LORA = 128
ROPE_AXIS = QK_ROPE // 2
ROPE_THETA = 10000.0
FNO_GROUPS = 8
FNO_GROUP_DIM = 64
FNO_WIDTH = FNO_GROUPS * FNO_GROUP_DIM
N_RWKV_IN = 3 * RWKV_WIDTH + N_DIR * DECAY_LORA + N_DIR * AAA_LORA + GATE_LORA
N_MLA_IN = Q_LORA + KV_LORA + QK_ROPE
RMS_EPS = 1e-6

LANES = 128
CHUNK = 64
QK_PAD = LANES
ROW_TILE = 1024
ZR_PAD = 2048
ZM_PAD = 1024
COL_TILE = 1024
FFN_CHUNK = 256
VMEM_LIMIT = 48 * 1024 * 1024


def _params(*sem):
    return pltpu.CompilerParams(dimension_semantics=sem, vmem_limit_bytes=VMEM_LIMIT)


def _dot(a, b):
    return jnp.dot(a, b, preferred_element_type=F32)


def _dot_nt(a, b):
    return lax.dot_general(a, b, (((1,), (1,)), ((), ())), preferred_element_type=F32)


def _dot_tn(a, b):
    return lax.dot_general(a, b, (((0,), (0,)), ((), ())), preferred_element_type=F32)


def _sigmoid(x):
    return 1.0 / (1.0 + jnp.exp(-x))


def _pick(dim, pref):
    t = min(dim, pref)
    while dim % t:
        t //= 2
    return t


def _mm_f32_body(a_ref, b_ref, bias_ref, o_ref):
    o_ref[...] = jnp.dot(a_ref[...], b_ref[...], preferred_element_type=F32,
                         precision=lax.Precision.HIGHEST) + bias_ref[...]


def matmul_f32_bias(a, b, bias, tn=512):
    M, K = a.shape
    _, N = b.shape
    return pl.pallas_call(
        _mm_f32_body,
        grid=(N // tn,),
        in_specs=[pl.BlockSpec((M, K), lambda j: (0, 0)),
                  pl.BlockSpec((K, tn), lambda j: (0, j)),
                  pl.BlockSpec((1, tn), lambda j: (0, j))],
        out_specs=pl.BlockSpec((M, tn), lambda j: (0, j)),
        out_shape=jax.ShapeDtypeStruct((M, N), F32),
        compiler_params=_params("parallel"),
        name="mm_f32_bias",
    )(a, b, bias.reshape(1, N))


def _norm_mod(x, gain, shift, scale):
    y = x * lax.rsqrt(jnp.mean(x * x, axis=-1, keepdims=True) + RMS_EPS)
    return (y * gain) * (1.0 + scale) + shift


def _inproj_body(x_ref, mod_ref, g_ref, wr_ref, wm_ref, wg_ref, wd_ref, zr_ref, zm_ref, zg_ref, zd_ref,
                 h_ref):
    j = pl.program_id(1)

    @pl.when(j == 0)
    def _():
        h_ref[...] = _norm_mod(x_ref[...], g_ref[...], mod_ref[0:1, :], mod_ref[1:2, :]).astype(BF16)

    @pl.when(j < 2)
    def _():
        zr_ref[...] = _dot(h_ref[...], wr_ref[...])

    @pl.when(j == 2)
    def _():
        zm_ref[...] = _dot(h_ref[...], wm_ref[...])

    @pl.when((j >= 3) & (j < 6))
    def _():
        zg_ref[...] = _sigmoid(_dot(h_ref[...], wg_ref[...])).astype(BF16)

    @pl.when(j == 6)
    def _():
        zd_ref[...] = _dot(h_ref[...], wd_ref[...]).astype(BF16)


def _mod_row(i, tiles_per_sample, n_samples):
    return jnp.minimum(i // tiles_per_sample, n_samples)


def inproj(xg, modtab, gain, w_r, w_m, w_g, w_d, n_lat):
    M, D = xg.shape
    n_samples = modtab.shape[0] - 1
    tps = n_lat // ROW_TILE
    clamp = lambda j, lo, hi: jnp.minimum(jnp.maximum(j, lo), hi)
    zr_col = lambda i, j: clamp(j, 0, 1)
    zg_col = lambda i, j: clamp(j - 3, 0, 2)
    first = lambda i, j: 0
    wtile = lambda col: pl.BlockSpec((D, COL_TILE), lambda i, j: (0, col(i, j)))
    otile = lambda col: pl.BlockSpec((ROW_TILE, COL_TILE), lambda i, j: (i, col(i, j)))
    return pl.pallas_call(
        _inproj_body,
        grid=(M // ROW_TILE, 7),
        in_specs=[pl.BlockSpec((ROW_TILE, D), lambda i, j: (i, 0)),
                  pl.BlockSpec((None, 6, D), lambda i, j: (_mod_row(i, tps, n_samples), 0, 0)),
                  pl.BlockSpec((1, D), lambda i, j: (0, 0)),
                  wtile(zr_col), wtile(first), wtile(zg_col), wtile(first)],
        out_specs=[otile(zr_col), otile(first), otile(zg_col), otile(first)],
        out_shape=[jax.ShapeDtypeStruct((M, ZR_PAD), F32),
                   jax.ShapeDtypeStruct((M, ZM_PAD), F32),
                   jax.ShapeDtypeStruct((M, 3 * D), BF16),
                   jax.ShapeDtypeStruct((M, 2 * FNO_WIDTH), BF16)],
        scratch_shapes=[pltpu.VMEM((ROW_TILE, D), BF16)],
        compiler_params=_params("parallel", "arbitrary"),
        name="inproj",
    )(xg, modtab, gain.reshape(1, D), w_r, w_m, w_g, w_d)


def _head_sum(t, hs, terms=1):
    parts = _split3(t)[:terms]
    return sum(_dot(part, hs) for part in parts)


def _slot_tile(i, tm, n_lat, n_ctx, n_samples, nb):
    ng = n_samples // nb
    tps, lat_tiles = n_lat // tm, n_samples * n_lat // tm
    ctx_tiles = ng * n_ctx // tm
    sample = i // tps
    c = i - lat_tiles
    slot = jnp.where(i < lat_tiles, sample // ng, c // ctx_tiles)
    local = jnp.where(i < lat_tiles, (sample % ng) * tps + i % tps, ng * tps + c % ctx_tiles)
    return slot, local


def _post_body(*refs, nb, slot_of):
    y_refs = refs[:2 * nb]
    (bonus_ref, gate_ref, gnw_ref, gnb_ref, hs_ref, att_ref, four_ref, zg_ref, x_ref, mod_ref,
     wr_ref, wm_ref, wf_ref, wo_ref, out_ref) = refs[2 * nb:]
    D = x_ref.shape[1]
    slot = slot_of(pl.program_id(0))
    y = y_refs[0][...] + y_refs[nb][...]
    for j in range(1, nb):
        y = jnp.where(slot == j, y_refs[j][...] + y_refs[nb + j][...], y)
    hs = hs_ref[...]
    inv_n = 1.0 / RWKV_HEAD_DIM
    dev = y - _head_sum(y, hs, terms=2) * inv_n
    var = _head_sum(dev * dev, hs) * inv_n
    yn = dev * lax.rsqrt(var + GN_EPS) * gnw_ref[...] + gnb_ref[...]
    o = ((yn + bonus_ref[...]) * gate_ref[...]).astype(BF16)
    a = _dot(o, wr_ref[...])
    b = _dot(att_ref[...], wm_ref[...])
    f = _dot(four_ref[...], wf_ref[...])
    merged = (zg_ref[:, 0:D].astype(F32) * a + zg_ref[:, D:2 * D].astype(F32) * b
              + zg_ref[:, 2 * D:3 * D].astype(F32) * f)
    out_ref[...] = x_ref[...] + mod_ref[2:3, :] * _dot(merged.astype(BF16), wo_ref[...])


def post(y_slots, bonus, gate, gn_w, gn_b, hs, att, four, zg, xg, modtab, w_r, w_m, w_f, w_o,
         n_rows, n_lat, n_ctx, tm=512):
    D = xg.shape[1]
    W = bonus.shape[1]
    n_samples = modtab.shape[0] - 1
    nb = len(y_slots) // N_DIR
    assert (n_samples // nb * n_ctx) % tm == 0 and n_lat % tm == 0
    tps = n_lat // tm
    slot_tile = functools.partial(_slot_tile, tm=tm, n_lat=n_lat, n_ctx=n_ctx, n_samples=n_samples, nb=nb)
    row = lambda i: (i, 0)
    whole = lambda i: (0, 0)
    rows = lambda width: pl.BlockSpec((tm, width), row)
    wspec = pl.BlockSpec((W, D), whole)
    vec = pl.BlockSpec((1, W), whole)

    def slot_spec(j):
        def index(i):
            slot, local = slot_tile(i)
            return (jnp.where(slot == j, local, 0), 0)
        return pl.BlockSpec((tm, W), index)

    return pl.pallas_call(
        functools.partial(_post_body, nb=nb, slot_of=lambda i: slot_tile(i)[0]),
        grid=(n_rows // tm,),
        in_specs=[slot_spec(j) for _ in range(N_DIR) for j in range(nb)]
                 + [rows(W), rows(W), vec, vec, pl.BlockSpec((W, W), whole),
                    rows(att.shape[1]), rows(four.shape[1]), rows(3 * D), rows(D),
                    pl.BlockSpec((None, 6, D), lambda i: (_mod_row(i, tps, n_samples), 0, 0)),
                    wspec, wspec, wspec, pl.BlockSpec((D, D), whole)],
        out_specs=rows(D),
        out_shape=jax.ShapeDtypeStruct((n_rows, D), F32),
        compiler_params=_params("parallel"),
        name="post",
    )(*y_slots, bonus, gate, gn_w.reshape(1, W), gn_b.reshape(1, W), hs, att, four, zg, xg, modtab,
      w_r.astype(BF16), w_m.astype(BF16), w_f.astype(BF16), w_o.astype(BF16))


def _feat_body(z_ref, zp_ref, zn_ref, mup_ref, mun_ref, w2_ref, a2_ref, g2_ref, w0_ref, a0_ref, kk_ref,
               ka_ref, rk_ref, hs_ref, r_ref, v_ref, kkn_ref, lw_ref, kd_ref, bd_ref, gate_ref, bonus_ref,
               *, n_lat_rows, seg_lat, seg_ctx):
    tm = z_ref.shape[0]
    nz = N_RWKV_IN
    W = RWKV_WIDTH
    i = pl.program_id(0)
    z = z_ref[:, :nz]
    loc = lax.broadcasted_iota(jnp.int32, (tm, 1), 0)
    row = loc + i * tm
    is_lat = row < n_lat_rows
    pos = jnp.where(is_lat, row & (seg_lat - 1), (row - n_lat_rows) & (seg_ctx - 1))
    seg_last = jnp.where(is_lat, seg_lat - 1, seg_ctx - 1)
    z_prev = jnp.where(loc == 0, zp_ref[7:8, :nz], pltpu.roll(z, 1, 0))
    z_prev = jnp.where(pos == 0, 0.0, z_prev)
    z_next = jnp.where(loc == tm - 1, zn_ref[0:1, :nz], pltpu.roll(z, tm - 1, 0))
    z_next = jnp.where(pos == seg_last, 0.0, z_next)
    z = z + (z_prev - z) * mup_ref[...] + (z_next - z) * mun_ref[...]

    r, k, v = z[:, :W], z[:, W:2 * W], z[:, 2 * W:3 * W]
    zw = jnp.tanh(z[:, 3 * W:3 * W + LANES]).astype(BF16)
    za = z[:, 3 * W + LANES:3 * W + 2 * LANES].astype(BF16)
    zg = _sigmoid(z[:, 3 * W + 2 * LANES:3 * W + 3 * LANES]).astype(BF16)
    lw = (-math.exp(-0.5)) * _sigmoid(_dot(zw, w2_ref[...]) + w0_ref[...])
    a = _sigmoid(_dot(za, a2_ref[...]) + a0_ref[...])
    hs = hs_ref[...]
    kq = k * kk_ref[...]
    kkn = kq * lax.rsqrt(jnp.maximum(_head_sum(kq * kq, hs), 1e-24))
    k_sum = None
    for d in range(N_DIR):
        a_d = a[:, d * W:(d + 1) * W]
        k_d = k * (1.0 + (a_d - 1.0) * ka_ref[...])
        lw_ref[d] = lw[:, d * W:(d + 1) * W]
        kd_ref[d] = k_d
        bd_ref[d] = kkn * a_d
        k_sum = k_d if k_sum is None else k_sum + k_d
    r_ref[...] = r
    v_ref[...] = v.astype(v_ref.dtype)
    kkn_ref[...] = kkn
    gate_ref[...] = _dot(zg, g2_ref[...]).astype(gate_ref.dtype)
    bonus_ref[...] = (_head_sum(r * rk_ref[...] * k_sum, hs) * v).astype(bonus_ref.dtype)


def rwkv_features(zr, p, hs, n_samples, n_lat, n_ctx, tm=256):
    M, ZW = zr.shape
    W = RWKV_WIDTH
    assert n_lat & (n_lat - 1) == 0 and n_ctx & (n_ctx - 1) == 0 and n_lat % tm == 0 and n_ctx % tm == 0
    pad = lambda t: t.reshape(1, N_RWKV_IN)
    zero = jnp.zeros((DECAY_LORA, W), F32)
    blockdiag = lambda t: jnp.concatenate(
        [jnp.concatenate([t[0], zero], axis=1), jnp.concatenate([zero, t[1]], axis=1)], axis=0).astype(BF16)
    row = lambda i: (i, 0)
    whole = lambda i: (0, 0)
    halo = tm // 8
    full = lambda t: pl.BlockSpec(t.shape, whole)
    params = [pad(p["mu_prev"]), pad(p["mu_next"]), blockdiag(p["w2"]), blockdiag(p["a2"]),
              p["g2"].astype(BF16), p["w0"].reshape(1, N_DIR * W), p["a0"].reshape(1, N_DIR * W),
              p["k_k"].reshape(1, W), p["k_a"].reshape(1, W), p["r_k"].reshape(1, W), hs]
    one = pl.BlockSpec((tm, W), row)
    two = pl.BlockSpec((N_DIR, tm, W), lambda i: (0, i, 0))
    s1 = jax.ShapeDtypeStruct((M, W), F32)
    sb = jax.ShapeDtypeStruct((M, W), BF16)
    s2 = jax.ShapeDtypeStruct((N_DIR, M, W), F32)
    return pl.pallas_call(
        functools.partial(_feat_body, n_lat_rows=n_samples * n_lat, seg_lat=n_lat, seg_ctx=n_ctx),
        grid=(M // tm,),
        in_specs=[pl.BlockSpec((tm, ZW), row),
                  pl.BlockSpec((8, ZW), lambda i: (jnp.maximum(i * halo - 1, 0), 0)),
                  pl.BlockSpec((8, ZW), lambda i: (jnp.minimum((i + 1) * halo, M // 8 - 1), 0))]
                 + [full(t) for t in params],
        out_specs=[one, one, one, two, two, two, one, one],
        out_shape=[s1, sb, s1, s2, s2, s2, sb, sb],
        compiler_params=_params("parallel"),
        name="rwkv_features",
    )(zr, zr, zr, *params)


def _ffn_body(x_ref, mod_ref, g_ref, wg_ref, wu_ref, wo_ref, out_ref, h_ref, acc_ref, *, n_chunks):
    c = pl.program_id(1)

    @pl.when(c == 0)
    def _():
        h_ref[...] = _norm_mod(x_ref[...], g_ref[...], mod_ref[3:4, :], mod_ref[4:5, :]).astype(BF16)
        acc_ref[...] = jnp.zeros_like(acc_ref)

    h = h_ref[...]
    gate = _dot(h, wg_ref[...])
    up = _dot(h, wu_ref[...])
    act = gate * _sigmoid(gate) * up
    acc_ref[...] += _dot(act.astype(BF16), wo_ref[...])

    @pl.when(c == n_chunks - 1)
    def _():
        out_ref[...] = x_ref[...] + mod_ref[5:6, :] * acc_ref[...]


def ffn(xg, modtab, gain, w_in, w_out, n_lat):
    M, D = xg.shape
    hidden = w_out.shape[0]
    n_samples = modtab.shape[0] - 1
    tps = n_lat // ROW_TILE
    assert hidden % FFN_CHUNK == 0
    n_chunks = hidden // FFN_CHUNK
    w_in = w_in.astype(BF16)
    w_out = w_out.astype(BF16)
    return pl.pallas_call(
        functools.partial(_ffn_body, n_chunks=n_chunks),
        grid=(M // ROW_TILE, n_chunks),
        in_specs=[pl.BlockSpec((ROW_TILE, D), lambda i, c: (i, 0)),
                  pl.BlockSpec((None, 6, D), lambda i, c: (_mod_row(i, tps, n_samples), 0, 0)),
                  pl.BlockSpec((1, D), lambda i, c: (0, 0)),
                  pl.BlockSpec((D, FFN_CHUNK), lambda i, c: (0, c)),
                  pl.BlockSpec((D, FFN_CHUNK), lambda i, c: (0, n_chunks + c)),
                  pl.BlockSpec((FFN_CHUNK, D), lambda i, c: (c, 0))],
        out_specs=pl.BlockSpec((ROW_TILE, D), lambda i, c: (i, 0)),
        out_shape=jax.ShapeDtypeStruct((M, D), F32),
        scratch_shapes=[pltpu.VMEM((ROW_TILE, D), BF16), pltpu.VMEM((ROW_TILE, D), F32)],
        compiler_params=_params("parallel", "arbitrary"),
        name="ffn",
    )(xg, modtab, gain.reshape(1, D), w_in, w_in, w_out)


def _dft_body(w_ref, z_ref, o_ref, acc_ref, *, nk, scale):
    k = pl.program_id(2)

    @pl.when(k == 0)
    def _():
        acc_ref[...] = jnp.zeros_like(acc_ref)

    acc_ref[...] += _dot(w_ref[...], z_ref[...])

    @pl.when(k == nk - 1)
    def _():
        o_ref[...] = (acc_ref[...] * scale).astype(o_ref.dtype)


def time_dft(w, z, n_samples, row0, scale, tm=1024, tk=2048):
    n = w.shape[0]
    Wd = z.shape[1] // 2
    tm, tk = _pick(n, tm), _pick(n, tk)
    nkh = n // tk
    nk = 2 * nkh
    assert row0 % tk == 0
    off = row0 // tk
    return pl.pallas_call(
        functools.partial(_dft_body, nk=nk, scale=scale),
        grid=(n_samples, n // tm, nk),
        in_specs=[pl.BlockSpec((tm, tk), lambda b, i, k: (i, k)),
                  pl.BlockSpec((tk, Wd), lambda b, i, k: (off + b * nkh + k % nkh, k // nkh))],
        out_specs=pl.BlockSpec((tm, Wd), lambda b, i, k: (b * (n // tm) + i, 0)),
        out_shape=jax.ShapeDtypeStruct((n_samples * n, Wd), BF16),
        scratch_shapes=[pltpu.VMEM((tm, Wd), F32)],
        compiler_params=_params("parallel", "parallel", "arbitrary"),
        name="time_dft",
    )(w, z)


def _delta_chains(chains, m_a, bdmask):
    C = CHUNK
    bf = lambda t: t.astype(BF16)
    zero = jnp.zeros((), BF16)
    cat = lambda ts: jnp.concatenate(ts, axis=0)

    def blk(t):
        t = bf(t)
        return cat([jnp.where(m_a, t, zero), jnp.where(m_a, zero, t)])

    a4 = [_dot_nt(cat([bf(c["ah"]), bf(c["rh"])]), cat([blk(c["bh"]), blk(c["kh"])])) for c in chains]
    a_ab = [jnp.where(c["strict2"], a[:C, :2 * C], 0.0) for c, a in zip(chains, a4)]
    a_ak = [jnp.where(c["strict2"], a[:C, 2 * C:], 0.0) for c, a in zip(chains, a4)]
    a_r = [jnp.where(c["incl4"], a[C:], 0.0) for c, a in zip(chains, a4)]
    x = a_ab
    pw = a_ab
    pwb = [blk(t) for t in pw]
    for _ in range(int(math.log2(C)) - 1):
        pw = [_dot(bf(t), tb) for t, tb in zip(pw, pwb)]
        pwb = [blk(t) for t in pw]
        x = [xi + t + _dot(bf(xi), tb) for xi, t, tb in zip(x, pw, pwb)]
    spb = [bf(c["sp"]) for c in chains]
    vb = [blk(c["v"]) for c in chains]
    x1 = [_dot_nt(bf(c["ah"]), s) + _dot(bf(a), v) for c, s, a, v in zip(chains, spb, a_ak, vb)]
    u = [t + _dot(bf(xi), blk(t)) for t, xi in zip(x1, x)]
    y = [_dot_nt(bf(c["rh"]), s) + _dot(bf(a), cat([blk(ui), v]))
         for c, s, a, ui, v in zip(chains, spb, a_r, u, vb)]
    upd = [_dot_tn(cat([bf(ui), bf(c["v"])]), cat([bf(c["bb"]), bf(c["kb"])])) for c, ui in zip(chains, u)]
    sp_new = [c["sp"] * c["dtot"] + jnp.where(bdmask, t, 0.0) for c, t in zip(chains, upd)]
    return list(zip(y, sp_new))


def _split3(t):
    hi = t.astype(BF16)
    r1 = t - hi.astype(F32)
    mid = r1.astype(BF16)
    lo = (r1 - mid.astype(F32)).astype(BF16)
    return hi, mid, lo


def _delta_prep(r, lw, k, v, kk, b, rev):
    C = CHUNK

    def causal(width, strict):
        ti = lax.broadcasted_iota(jnp.int32, (C, width), 0)
        ii = lax.broadcasted_iota(jnp.int32, (C, width), 1) & (C - 1)
        if rev:
            return ii > ti if strict else ii >= ti
        return ii < ti if strict else ii <= ti

    tri = causal(C, False).astype(BF16)
    L = sum(_dot(tri, part) for part in _split3(lw))
    last = 0 if rev else C - 1
    ltot = L[last:last + 1, :]
    e_nl = jnp.exp(-L)
    e_t = jnp.exp(ltot - L)
    return dict(ah=-kk * jnp.exp(L - lw), rh=r * jnp.exp(L), bh=b * e_nl, kh=k * e_nl, bb=b * e_t,
                kb=k * e_t, v=v, dtot=jnp.exp(ltot)), causal(2 * C, True), causal(4 * C, False)


def _delta_body(*refs, nb):
    ins, outs, s_ref = refs[:12 * nb], refs[12 * nb:14 * nb], refs[14 * nb]

    @pl.when(pl.program_id(1) == 0)
    def _():
        s_ref[...] = jnp.zeros_like(s_ref)

    C, W = ins[0].shape
    m_a = lax.broadcasted_iota(jnp.int32, (C, LANES), 1) < (LANES // 2)
    br = lax.broadcasted_iota(jnp.int32, (LANES, LANES), 0) < (LANES // 2)
    bc = lax.broadcasted_iota(jnp.int32, (LANES, LANES), 1) < (LANES // 2)
    bdmask = br == bc
    chains, dest = [], []
    for d in range(N_DIR):
        for j in range(nb):
            r_ref, v_ref, kk_ref, lw_ref, k_ref, b_ref = ins[(d * nb + j) * 6:(d * nb + j + 1) * 6]
            ops, strict2, incl4 = _delta_prep(r_ref[...], lw_ref[...], k_ref[...], v_ref[...], kk_ref[...],
                                              b_ref[...], rev=(d == 1))
            for p in range(W // LANES):
                sl = slice(p * LANES, (p + 1) * LANES)
                c = {name: t[:, sl] for name, t in ops.items()}
                c.update(sp=s_ref[d, j, p], strict2=strict2, incl4=incl4)
                chains.append(c)
                dest.append((outs[d * nb + j], d, j, p, sl))
    for (y, sp_new), (y_ref, d, j, p, sl) in zip(_delta_chains(chains, m_a, bdmask), dest):
        y_ref[:, sl] = y
        s_ref[d, j, p] = sp_new


def rwkv_scan(r, v, kk, lw, k, b, n_samples, n_lat, n_ctx, samples=2):
    M, W = r.shape
    ncl, ncc = n_lat // CHUNK, n_ctx // CHUNK
    nc = ncl + ncc
    nb = _pick(n_samples, samples)
    ng = n_samples // nb

    def row_block(n_s, s, t):
        return jnp.where(t < ncc, n_s * ncl + s * ncc + t, s * ncl + (t - ncc))

    def step_chunk(d, c):
        if d == 0:
            return c
        return jnp.where(c < ncc, ncc - 1 - c, nc + ncc - 1 - c)

    in_specs, out_specs, args = [], [], []
    for d in range(N_DIR):
        for j in range(nb):
            def rows(bi, c, d=d, j=j):
                return row_block(n_samples, j * ng + bi, step_chunk(d, c))

            def out_rows(bi, c, d=d):
                return row_block(ng, bi, step_chunk(d, c))
            shared = pl.BlockSpec((CHUNK, W), lambda bi, c, rows=rows: (rows(bi, c), 0))
            per_dir = pl.BlockSpec((None, CHUNK, W), lambda bi, c, rows=rows, d=d: (d, rows(bi, c), 0))
            in_specs += [shared] * 3 + [per_dir] * 3
            args += [r, v, kk, lw, k, b]
            out_specs.append(pl.BlockSpec((CHUNK, W), lambda bi, c, out_rows=out_rows: (out_rows(bi, c), 0)))
    outs = pl.pallas_call(
        functools.partial(_delta_body, nb=nb),
        grid=(ng, nc),
        in_specs=in_specs,
        out_specs=out_specs,
        out_shape=[jax.ShapeDtypeStruct((M // nb, W), F32)] * (N_DIR * nb),
        scratch_shapes=[pltpu.VMEM((N_DIR, nb, W // LANES, LANES, LANES), F32)],
        compiler_params=_params("parallel", "arbitrary"),
        name="delta_scan",
    )(*args)
    return outs


VT_ROWS = V_HEAD + 16
ATTN_LOOKAHEAD = 6


def _attn_body(*refs, n_seg, kv_chunk):
    q_ref, kv, o_ref = refs[0], refs[1:1 + 2 * n_seg], refs[1 + 2 * n_seg]
    items = []
    for g in range(n_seg):
        n = kv[2 * g].shape[0]
        ck = min(kv_chunk, n)
        items += [(hh, g, c0, ck) for c0 in range(0, n, ck) for hh in range(2)]
    q = [q_ref[:, hh * QK_PAD:(hh + 1) * QK_PAD] for hh in range(2)]
    m = [None, None]
    acc = [None, None]
    scores = {}
    for idx in range(len(items) + ATTN_LOOKAHEAD):
        if idx < len(items):
            hh, g, c0, ck = items[idx]
            scores[idx] = _dot_nt(kv[2 * g][c0:c0 + ck, hh * QK_PAD:(hh + 1) * QK_PAD], q[hh])
        if idx >= ATTN_LOOKAHEAD:
            hh, g, c0, ck = items[idx - ATTN_LOOKAHEAD]
            st = scores.pop(idx - ATTN_LOOKAHEAD)
            mc = jnp.max(st, axis=0, keepdims=True)
            m_new = mc if m[hh] is None else jnp.maximum(m[hh], mc)
            pt = jnp.exp2(st - m_new).astype(BF16)
            pv = _dot(kv[2 * g + 1][hh * QK_PAD:hh * QK_PAD + VT_ROWS, c0:c0 + ck], pt)
            acc[hh] = pv if m[hh] is None else acc[hh] * jnp.exp2(m[hh] - m_new) + pv
            m[hh] = m_new
    outs = [a[:V_HEAD] / a[V_HEAD:V_HEAD + 1] for a in acc]
    o_ref[...] = jnp.concatenate(outs, axis=0).T.astype(o_ref.dtype)


def attention(q, k, vt, n_samples, q_row0, n_q, segments, tq=256, kv_chunk=256):
    H = q.shape[1] // QK_PAD
    tq = _pick(n_q, tq)
    assert q_row0 % tq == 0
    nq = n_q // tq
    in_specs = [pl.BlockSpec((tq, 2 * QK_PAD), lambda b, h, i: (q_row0 // tq + b * nq + i, h))]
    args = [q]
    for row0, n in segments:
        assert row0 % n == 0
        in_specs += [pl.BlockSpec((n, 2 * QK_PAD), lambda b, h, i, row0=row0, n=n: (row0 // n + b, h)),
                     pl.BlockSpec((2 * QK_PAD, n), lambda b, h, i, row0=row0, n=n: (h, row0 // n + b))]
        args += [k, vt]
    return pl.pallas_call(
        functools.partial(_attn_body, n_seg=len(segments), kv_chunk=kv_chunk),
        grid=(n_samples, H // 2, nq),
        in_specs=in_specs,
        out_specs=pl.BlockSpec((tq, 2 * V_HEAD), lambda b, h, i: (b * nq + i, h)),
        out_shape=jax.ShapeDtypeStruct((n_samples * n_q, H * V_HEAD), BF16),
        compiler_params=_params("parallel", "parallel", "parallel"),
        name="attention",
    )(*args)


def _lane_sum(t, ones):
    tb = t.astype(BF16)
    return jnp.concatenate([_dot(tb[:, c:c + LANES], ones) for c in range(0, t.shape[1], LANES)], axis=1)


def _mla_body(z_ref, gcq_ref, gckv_ref, wq_ref, wk_ref, wv_ref, gq_ref, gk_ref, cos_ref, sin_ref, ones_ref,
              q_ref, k_ref, vt_ref, *, q_scale):
    tm = z_ref.shape[0]
    H = MLA_HEADS
    ones = ones_ref[...]
    c_q = z_ref[:, :Q_LORA]
    c_kv = z_ref[:, Q_LORA:Q_LORA + KV_LORA]
    cq_n = c_q * lax.rsqrt(jnp.mean(c_q * c_q, axis=-1, keepdims=True) + RMS_EPS) * gcq_ref[...]
    ckv_n = (c_kv * lax.rsqrt(jnp.mean(c_kv * c_kv, axis=-1, keepdims=True) + RMS_EPS)
             * gckv_ref[...]).astype(BF16)
    lane = lax.broadcasted_iota(jnp.int32, (tm, LANES), 1)
    k_rope = pltpu.roll(jnp.where(lane < QK_ROPE, z_ref[:, Q_LORA + KV_LORA:Q_LORA + KV_LORA + LANES], 0.0),
                        QK_NOPE, 1)
    q = _dot(cq_n.astype(BF16), wq_ref[...])
    k = _dot(ckv_n, wk_ref[...]) + jnp.concatenate([k_rope] * H, axis=1)
    cos = jnp.concatenate([cos_ref[...]] * H, axis=1)
    sin = jnp.concatenate([sin_ref[...]] * H, axis=1)
    first_half = (lax.broadcasted_iota(jnp.int32, (tm, H * LANES), 1) & (ROPE_AXIS // 2)) == 0

    def finish(t, gain, scale):
        t = t * lax.rsqrt(_lane_sum(t * t, ones) * (1.0 / QK_HEAD) + RMS_EPS) * gain
        partner = jnp.where(first_half, pltpu.roll(t, H * LANES - ROPE_AXIS // 2, 1),
                            pltpu.roll(t, ROPE_AXIS // 2, 1))
        return ((t * cos + partner * sin) * scale).astype(BF16)

    q_ref[...] = finish(q, gq_ref[...], q_scale)
    k_ref[...] = finish(k, gk_ref[...], 1.0)
    v = _dot(ckv_n, wv_ref[...])
    ones_rows = (lax.broadcasted_iota(jnp.int32, (tm, H * LANES), 1) & (LANES - 1)) >= V_HEAD
    vt_ref[...] = jnp.where(ones_rows, 1.0, v).T.astype(BF16)


def mla_prep(zm, p, rope_cos, rope_sin, n_lat_rows, n_lat, tm=256):
    M, ZW = zm.shape
    H = MLA_HEADS
    hpad = lambda t, n: jnp.pad(t, ((0, 0), (0, 0), (0, QK_PAD - n))).reshape(t.shape[0], H * QK_PAD)
    w_uq = hpad(p["w_uq"].reshape(Q_LORA, H, QK_HEAD), QK_HEAD).astype(BF16)
    w_ukv = p["w_ukv"].reshape(KV_LORA, H, QK_NOPE + V_HEAD)
    w_uk = hpad(w_ukv[..., :QK_NOPE], QK_NOPE).astype(BF16)
    w_uv = hpad(w_ukv[..., QK_NOPE:], V_HEAD).astype(BF16)
    gain = lambda g: jnp.tile(jnp.pad(g, (0, QK_PAD - QK_HEAD)), H).reshape(1, H * QK_PAD)
    ones = jnp.ones((LANES, LANES), BF16)
    consts = [p["g_cq"].reshape(1, Q_LORA), p["g_ckv"].reshape(1, KV_LORA), w_uq, w_uk, w_uv,
              gain(p["g_qn"]), gain(p["g_kn"])]
    row = lambda i: (i, 0)
    whole = lambda i: (0, 0)
    full = lambda t: pl.BlockSpec(t.shape, whole)
    lat_tiles, per_sample = n_lat_rows // tm, n_lat // tm
    tab = lambda i: (jnp.where(i < lat_tiles, i % per_sample, per_sample), 0)
    return pl.pallas_call(
        functools.partial(_mla_body, q_scale=QK_HEAD ** -0.5 * math.log2(math.e)),
        grid=(M // tm,),
        in_specs=[pl.BlockSpec((tm, ZW), row)] + [full(t) for t in consts]
                 + [pl.BlockSpec((tm, LANES), tab), pl.BlockSpec((tm, LANES), tab), full(ones)],
        out_specs=[pl.BlockSpec((tm, H * QK_PAD), row), pl.BlockSpec((tm, H * QK_PAD), row),
                   pl.BlockSpec((H * QK_PAD, tm), lambda i: (0, i))],
        out_shape=[jax.ShapeDtypeStruct((M, H * QK_PAD), BF16), jax.ShapeDtypeStruct((M, H * QK_PAD), BF16),
                   jax.ShapeDtypeStruct((H * QK_PAD, M), BF16)],
        compiler_params=_params("parallel"),
        name="mla_prep",
    )(zm, *consts, rope_cos, rope_sin, ones)


def _rope_lane_tables(n_tokens, tile):
    ang_row, ang_col = _rope_tables(n_tokens)
    one = jnp.ones((n_tokens, QK_NOPE), F32)
    zero = jnp.zeros((n_tokens, QK_NOPE), F32)
    pad1 = jnp.ones((n_tokens, QK_PAD - QK_HEAD), F32)
    pad0 = jnp.zeros((n_tokens, QK_PAD - QK_HEAD), F32)
    cr, cc, sr, sc = jnp.cos(ang_row), jnp.cos(ang_col), jnp.sin(ang_row), jnp.sin(ang_col)
    cos = jnp.concatenate([one, cr, cr, cc, cc, pad1], axis=1)
    sin = jnp.concatenate([zero, -sr, sr, -sc, sc, pad0], axis=1)
    cos = jnp.concatenate([cos, jnp.ones((tile, QK_PAD), F32)], axis=0)
    sin = jnp.concatenate([sin, jnp.zeros((tile, QK_PAD), F32)], axis=0)
    return cos, sin


def _rope_tables(n_tokens):
    rows = n_tokens // GRID_W
    row = jnp.broadcast_to(jnp.arange(rows)[:, None], (rows, GRID_W)).reshape(-1).astype(F32)
    col = jnp.broadcast_to(jnp.arange(GRID_W)[None, :], (rows, GRID_W)).reshape(-1).astype(F32)
    inv_freq = ROPE_THETA ** (-jnp.arange(0, ROPE_AXIS, 2, dtype=F32) / ROPE_AXIS)
    return row[:, None] * inv_freq, col[:, None] * inv_freq


def _dft_time_matrix(n):
    f = int(round(math.sqrt(n)))
    assert f * f == n
    k = jnp.arange(n, dtype=jnp.int32)[:, None]
    t = jnp.arange(f, dtype=jnp.int32)[None, :]
    a = ((k * t) % f).astype(F32) * (2.0 * math.pi / f)
    b = ((k * t) % n).astype(F32) * (2.0 * math.pi / n)
    ca, sa = jnp.cos(a)[:, :, None], jnp.sin(a)[:, :, None]
    cb, sb = jnp.cos(b)[:, None, :], jnp.sin(b)[:, None, :]
    cos = (ca * cb - sa * sb).reshape(n, n)
    sin = (sa * cb + ca * sb).reshape(n, n)
    return jnp.concatenate([cos, -sin], axis=1).astype(BF16)


def _dft_channel_matrix():
    idx = jnp.arange(FNO_GROUP_DIM, dtype=jnp.int32)
    ang = ((idx[:, None] * idx[None, :]) % FNO_GROUP_DIM).astype(F32) * (2.0 * math.pi / FNO_GROUP_DIM)
    eye = jnp.eye(FNO_GROUPS, dtype=F32)
    return jnp.concatenate([jnp.kron(eye, jnp.cos(ang)), jnp.kron(eye, jnp.sin(ang))], axis=1)


def _layer(xg, c_act, cctx_act, p, last, consts, dims):
    B, S, Sc = dims
    M, D = xg.shape
    ML = B * S
    dft_x, dft_c, dft_ch, (rope_cos, rope_sin), head_ones = consts

    act = jnp.concatenate([c_act, cctx_act[None, :], jnp.zeros((7 - B, D), F32)], axis=0)
    modtab = matmul_f32_bias(act, p["w_mod"], p["b_mod"])[:B + 1].reshape(B + 1, 6, D)

    w_in = p["w_in"]
    o1 = N_RWKV_IN
    o2 = o1 + N_MLA_IN
    o3 = o2 + FNO_WIDTH
    w_fd = matmul_f32_bias(w_in[:, o2:o3], dft_ch, jnp.zeros((2 * FNO_WIDTH,), F32))
    zr, zm, zg, zd = inproj(xg, modtab, p["g_norm1"], w_in[:, :ZR_PAD].astype(BF16),
                            w_in[:, o1:o1 + ZM_PAD].astype(BF16), w_in[:, o3:].astype(BF16),
                            w_fd.astype(BF16), S)

    r, v, kkf, lw_dir, k_dir, b_dir, g_out, bonus = rwkv_features(zr, p, head_ones, B, S, Sc)
    y_slots = rwkv_scan(r, v, kkf, lw_dir, k_dir, b_dir, B, S, Sc)

    qh, kh, vh = mla_prep(zm, p, rope_cos, rope_sin, ML, S)
    att = attention(qh, kh, vh, B, 0, S, [(0, S), (ML, Sc)])

    yf = time_dft(dft_x, zd, B, 0, 1.0 / math.sqrt(S * FNO_GROUP_DIM))

    n_rows = ML if last else M
    if not last:
        att = jnp.concatenate([att, attention(qh, kh, vh, B, ML, Sc, [(ML, Sc)])], axis=0)
        yf = jnp.concatenate([yf, time_dft(dft_c, zd, B, ML, 1.0 / math.sqrt(Sc * FNO_GROUP_DIM))], axis=0)
    x_mid = post(y_slots, bonus, g_out, p["gn_w"], p["gn_b"], head_ones, att, yf, zg, xg, modtab,
                 p["w_rwkv_o"], p["w_mla_o"], p["w_fno"], p["w_out"], n_rows, S, Sc)
    return ffn(x_mid, modtab, p["g_norm2"], p["w_ffn_in"], p["w_ffn_out"], S)


def kernel(x, c, ctx, c_ctx, w_mod, b_mod, g_norm1, w_in, mu_prev, mu_next, w0, w2, a0, a2, k_k, k_a, r_k, g2, gn_w, gn_b, w_rwkv_o, g_cq, g_ckv, w_uq, w_ukv, g_qn, g_kn, w_mla_o, w_fno, w_out, g_norm2, w_ffn_in, w_ffn_out):
    layer_params = dict(w_mod=w_mod, b_mod=b_mod, g_norm1=g_norm1, w_in=w_in, mu_prev=mu_prev,
                        mu_next=mu_next, w0=w0, w2=w2, a0=a0, a2=a2, k_k=k_k, k_a=k_a, r_k=r_k,
                        g2=g2, gn_w=gn_w, gn_b=gn_b, w_rwkv_o=w_rwkv_o, g_cq=g_cq, g_ckv=g_ckv,
                        w_uq=w_uq, w_ukv=w_ukv, g_qn=g_qn, g_kn=g_kn, w_mla_o=w_mla_o, w_fno=w_fno,
                        w_out=w_out, g_norm2=g_norm2, w_ffn_in=w_ffn_in, w_ffn_out=w_ffn_out)
    depth = w_mod.shape[0]
    B, S, D = x.shape
    Sc = ctx.shape[1]
    assert S % ROW_TILE == 0 and (B * Sc) % ROW_TILE == 0 and B <= 7
    head_ones = jnp.kron(jnp.eye(RWKV_HEADS, dtype=F32),
                         jnp.ones((RWKV_HEAD_DIM, RWKV_HEAD_DIM), F32)).astype(BF16)
    consts = (_dft_time_matrix(S), _dft_time_matrix(Sc), _dft_channel_matrix(),
              _rope_lane_tables(S, 256), head_ones)
    c_act = jax.nn.silu(c)
    cctx_act = jax.nn.silu(c_ctx)
    xg = jnp.concatenate([x.reshape(B * S, D), ctx.reshape(B * Sc, D)], axis=0)
    for layer in range(depth):
        p = {name: arr[layer] for name, arr in layer_params.items()}
        xg = _layer(xg, c_act, cctx_act, p, layer == depth - 1, consts, (B, S, Sc))
    return xg[:B * S].reshape(B, S, D)
```

```python
import functools
import math

import jax
import jax.numpy as jnp
from jax import lax
from jax.experimental import pallas as pl
from jax.experimental.pallas import tpu as pltpu

F32 = jnp.float32
BF16 = jnp.bfloat16

D_MODEL = 1024
GRID_W = 64
RWKV_HEADS = 8
RWKV_HEAD_DIM = 64
RWKV_WIDTH = RWKV_HEADS * RWKV_HEAD_DIM
DECAY_LORA = 64
AAA_LORA = 64
GATE_LORA = 128
N_DIR = 2
GN_EPS = 64e-5
MLA_HEADS = 8
QK_NOPE = 64
QK_ROPE = 32
QK_HEAD = QK_NOPE + QK_ROPE
V_HEAD = 64
Q_LORA = 384
KV_LORA = 128
ROPE_AXIS = QK_ROPE // 2
ROPE_THETA = 10000.0
FNO_GROUPS = 8
FNO_GROUP_DIM = 64
FNO_WIDTH = FNO_GROUPS * FNO_GROUP_DIM
N_RWKV_IN = 3 * RWKV_WIDTH + N_DIR * DECAY_LORA + N_DIR * AAA_LORA + GATE_LORA
N_MLA_IN = Q_LORA + KV_LORA + QK_ROPE
RMS_EPS = 1e-6

LANES = 128
CHUNK = 64
QK_PAD = LANES
ROW_TILE = 1024
ZR_PAD = 2048
ZM_PAD = 1024
COL_TILE = 1024
FFN_CHUNK = 256
VMEM_LIMIT = 48 * 1024 * 1024


def _params(*sem):
    return pltpu.CompilerParams(dimension_semantics=sem, vmem_limit_bytes=VMEM_LIMIT)


def _dot(a, b):
    return jnp.dot(a, b, preferred_element_type=F32)


def _dot_nt(a, b):
    return lax.dot_general(a, b, (((1,), (1,)), ((), ())), preferred_element_type=F32)


def _dot_tn(a, b):
    return lax.dot_general(a, b, (((0,), (0,)), ((), ())), preferred_element_type=F32)


def _sigmoid(x):
    return 1.0 / (1.0 + jnp.exp(-x))


def _pick(dim, pref):
    t = min(dim, pref)
    while dim % t:
        t //= 2
    return t


def _mm_f32_body(a_ref, b_ref, bias_ref, o_ref):
    o_ref[...] = jnp.dot(a_ref[...], b_ref[...], preferred_element_type=F32,
                         precision=lax.Precision.HIGHEST) + bias_ref[...]


def _mm_bf16_body(a_ref, b_ref, o_ref):
    o_ref[...] = _dot(a_ref[...], b_ref[...]).astype(o_ref.dtype)


def matmul_bf16(a, b):
    return pl.pallas_call(
        _mm_bf16_body,
        out_shape=jax.ShapeDtypeStruct((a.shape[0], b.shape[1]), BF16),
        name="mm_bf16",
    )(a, b)


def matmul_f32_bias(a, b, bias, tn=512):
    M, K = a.shape
    _, N = b.shape
    return pl.pallas_call(
        _mm_f32_body,
        grid=(N // tn,),
        in_specs=[pl.BlockSpec((M, K), lambda j: (0, 0)),
                  pl.BlockSpec((K, tn), lambda j: (0, j)),
                  pl.BlockSpec((1, tn), lambda j: (0, j))],
        out_specs=pl.BlockSpec((M, tn), lambda j: (0, j)),
        out_shape=jax.ShapeDtypeStruct((M, N), F32),
        compiler_params=_params("parallel"),
        name="mm_f32_bias",
    )(a, b, bias.reshape(1, N))


def _norm_mod(x, gain, shift, scale):
    y = x * lax.rsqrt(jnp.mean(x * x, axis=-1, keepdims=True) + RMS_EPS)
    return (y * gain) * (1.0 + scale) + shift


def _inproj_body(x_ref, mod_ref, g_ref, wr_ref, wm_ref, wg_ref, wd_ref, zr_ref, zm_ref, zg_ref, zd_ref,
                 h_ref):
    j = pl.program_id(1)

    @pl.when(j == 0)
    def _():
        h_ref[...] = _norm_mod(x_ref[...], g_ref[...], mod_ref[0:1, :], mod_ref[1:2, :]).astype(BF16)

    @pl.when(j < 2)
    def _():
        zr_ref[...] = _dot(h_ref[...], wr_ref[...])

    @pl.when(j == 2)
    def _():
        zm_ref[...] = _dot(h_ref[...], wm_ref[...])

    @pl.when((j >= 3) & (j < 6))
    def _():
        zg_ref[...] = _sigmoid(_dot(h_ref[...], wg_ref[...])).astype(BF16)

    @pl.when(j == 6)
    def _():
        zd_ref[...] = _dot(h_ref[...], wd_ref[...]).astype(BF16)


def _mod_row(i, tiles_per_sample, n_samples):
    return jnp.minimum(i // tiles_per_sample, n_samples)


def inproj(xg, modtab, gain, w_r, w_m, w_g, w_d, n_lat):
    M, D = xg.shape
    n_samples = modtab.shape[0] - 1
    tps = n_lat // ROW_TILE
    clamp = lambda j, lo, hi: jnp.minimum(jnp.maximum(j, lo), hi)
    zr_col = lambda i, j: clamp(j, 0, 1)
    zg_col = lambda i, j: clamp(j - 3, 0, 2)
    first = lambda i, j: 0
    wtile = lambda col: pl.BlockSpec((D, COL_TILE), lambda i, j: (0, col(i, j)))
    otile = lambda col: pl.BlockSpec((ROW_TILE, COL_TILE), lambda i, j: (i, col(i, j)))
    return pl.pallas_call(
        _inproj_body,
        grid=(M // ROW_TILE, 7),
        in_specs=[pl.BlockSpec((ROW_TILE, D), lambda i, j: (i, 0)),
                  pl.BlockSpec((None, 6, D), lambda i, j: (_mod_row(i, tps, n_samples), 0, 0)),
                  pl.BlockSpec((1, D), lambda i, j: (0, 0)),
                  wtile(zr_col), wtile(first), wtile(zg_col), wtile(first)],
        out_specs=[otile(zr_col), otile(first), otile(zg_col), otile(first)],
        out_shape=[jax.ShapeDtypeStruct((M, ZR_PAD), F32),
                   jax.ShapeDtypeStruct((M, ZM_PAD), F32),
                   jax.ShapeDtypeStruct((M, 3 * D), BF16),
                   jax.ShapeDtypeStruct((M, 2 * FNO_WIDTH), BF16)],
        scratch_shapes=[pltpu.VMEM((ROW_TILE, D), BF16)],
        compiler_params=_params("parallel", "arbitrary"),
        name="inproj",
    )(xg, modtab, gain.reshape(1, D), w_r, w_m, w_g, w_d)


def _head_sum(t, hs, terms=1):
    parts = _split3(t)[:terms]
    return sum(_dot(part, hs) for part in parts)


def _slot_tile(i, tm, n_lat, n_ctx, n_samples, nb):
    ng = n_samples // nb
    tps, lat_tiles = n_lat // tm, n_samples * n_lat // tm
    ctx_tiles = ng * n_ctx // tm
    sample = i // tps
    c = i - lat_tiles
    slot = jnp.where(i < lat_tiles, sample // ng, c // ctx_tiles)
    local = jnp.where(i < lat_tiles, (sample % ng) * tps + i % tps, ng * tps + c % ctx_tiles)
    return slot, local


def _post_body(*refs, nb, slot_of):
    y_refs = refs[:2 * nb]
    (bonus_ref, gate_ref, gnw_ref, gnb_ref, hs_ref, att_ref, four_ref, zg_ref, x_ref, mod_ref,
     wr_ref, wm_ref, wf_ref, wo_ref, out_ref) = refs[2 * nb:]
    D = x_ref.shape[1]
    slot = slot_of(pl.program_id(0))
    y = y_refs[0][...] + y_refs[nb][...]
    for j in range(1, nb):
        y = jnp.where(slot == j, y_refs[j][...] + y_refs[nb + j][...], y)
    hs = hs_ref[...]
    inv_n = 1.0 / RWKV_HEAD_DIM
    dev = y - _head_sum(y, hs, terms=2) * inv_n
    var = _head_sum(dev * dev, hs) * inv_n
    yn = dev * lax.rsqrt(var + GN_EPS) * gnw_ref[...] + gnb_ref[...]
    o = ((yn + bonus_ref[...]) * gate_ref[...]).astype(BF16)
    a = _dot(o, wr_ref[...])
    b = _dot(att_ref[...], wm_ref[...])
    f = _dot(four_ref[...], wf_ref[...])
    merged = (zg_ref[:, 0:D].astype(F32) * a + zg_ref[:, D:2 * D].astype(F32) * b
              + zg_ref[:, 2 * D:3 * D].astype(F32) * f)
    out_ref[...] = x_ref[...] + mod_ref[2:3, :] * _dot(merged.astype(BF16), wo_ref[...])


def post(y_slots, bonus, gate, gn_w, gn_b, hs, att, four, zg, xg, modtab, w_r, w_m, w_f, w_o,
         n_rows, n_lat, n_ctx, tm=512):
    D = xg.shape[1]
    W = bonus.shape[1]
    n_samples = modtab.shape[0] - 1
    nb = len(y_slots) // N_DIR
    assert (n_samples // nb * n_ctx) % tm == 0 and n_lat % tm == 0
    tps = n_lat // tm
    slot_tile = functools.partial(_slot_tile, tm=tm, n_lat=n_lat, n_ctx=n_ctx, n_samples=n_samples, nb=nb)
    row = lambda i: (i, 0)
    whole = lambda i: (0, 0)
    rows = lambda width: pl.BlockSpec((tm, width), row)
    wspec = pl.BlockSpec((W, D), whole)
    vec = pl.BlockSpec((1, W), whole)

    def slot_spec(j):
        def index(i):
            slot, local = slot_tile(i)
            return (jnp.where(slot == j, local, 0), 0)
        return pl.BlockSpec((tm, W), index)

    return pl.pallas_call(
        functools.partial(_post_body, nb=nb, slot_of=lambda i: slot_tile(i)[0]),
        grid=(n_rows // tm,),
        in_specs=[slot_spec(j) for _ in range(N_DIR) for j in range(nb)]
                 + [rows(W), rows(W), vec, vec, pl.BlockSpec((W, W), whole),
                    rows(att.shape[1]), rows(four.shape[1]), rows(3 * D), rows(D),
                    pl.BlockSpec((None, 6, D), lambda i: (_mod_row(i, tps, n_samples), 0, 0)),
                    wspec, wspec, wspec, pl.BlockSpec((D, D), whole)],
        out_specs=rows(D),
        out_shape=jax.ShapeDtypeStruct((n_rows, D), F32),
        compiler_params=_params("parallel"),
        name="post",
    )(*y_slots, bonus, gate, gn_w.reshape(1, W), gn_b.reshape(1, W), hs, att, four, zg, xg, modtab,
      w_r.astype(BF16), w_m.astype(BF16), w_f.astype(BF16), w_o.astype(BF16))


def _feat_body(z_ref, zp_ref, zn_ref, mup_ref, mun_ref, w2_ref, a2_ref, g2_ref, w0_ref, a0_ref, kk_ref,
               ka_ref, rk_ref, hs_ref, r_ref, v_ref, kkn_ref, lw_ref, kd_ref, bd_ref, gate_ref, bonus_ref,
               *, n_lat_rows, seg_lat, seg_ctx):
    tm = z_ref.shape[0]
    nz = N_RWKV_IN
    W = RWKV_WIDTH
    i = pl.program_id(0)
    z = z_ref[:, :nz]
    loc = lax.broadcasted_iota(jnp.int32, (tm, 1), 0)
    row = loc + i * tm
    is_lat = row < n_lat_rows
    pos = jnp.where(is_lat, row & (seg_lat - 1), (row - n_lat_rows) & (seg_ctx - 1))
    seg_last = jnp.where(is_lat, seg_lat - 1, seg_ctx - 1)
    z_prev = jnp.where(loc == 0, zp_ref[7:8, :nz], pltpu.roll(z, 1, 0))
    z_prev = jnp.where(pos == 0, 0.0, z_prev)
    z_next = jnp.where(loc == tm - 1, zn_ref[0:1, :nz], pltpu.roll(z, tm - 1, 0))
    z_next = jnp.where(pos == seg_last, 0.0, z_next)
    z = z + (z_prev - z) * mup_ref[...] + (z_next - z) * mun_ref[...]

    r, k, v = z[:, :W], z[:, W:2 * W], z[:, 2 * W:3 * W]
    zw = jnp.tanh(z[:, 3 * W:3 * W + LANES]).astype(BF16)
    za = z[:, 3 * W + LANES:3 * W + 2 * LANES].astype(BF16)
    zg = _sigmoid(z[:, 3 * W + 2 * LANES:3 * W + 3 * LANES]).astype(BF16)
    lw = (-math.exp(-0.5)) * _sigmoid(_dot(zw, w2_ref[...]) + w0_ref[...])
    a = _sigmoid(_dot(za, a2_ref[...]) + a0_ref[...])
    hs = hs_ref[...]
    kq = k * kk_ref[...]
    kkn = kq * lax.rsqrt(jnp.maximum(_head_sum(kq * kq, hs), 1e-24))
    k_sum = None
    for d in range(N_DIR):
        a_d = a[:, d * W:(d + 1) * W]
        k_d = k * (1.0 + (a_d - 1.0) * ka_ref[...])
        lw_ref[d] = lw[:, d * W:(d + 1) * W]
        kd_ref[d] = k_d
        bd_ref[d] = kkn * a_d
        k_sum = k_d if k_sum is None else k_sum + k_d
    r_ref[...] = r
    v_ref[...] = v.astype(v_ref.dtype)
    kkn_ref[...] = kkn
    gate_ref[...] = _dot(zg, g2_ref[...]).astype(gate_ref.dtype)
    bonus_ref[...] = (_head_sum(r * rk_ref[...] * k_sum, hs) * v).astype(bonus_ref.dtype)


def rwkv_features(zr, p, hs, n_samples, n_lat, n_ctx, tm=256):
    M, ZW = zr.shape
    W = RWKV_WIDTH
    assert n_lat & (n_lat - 1) == 0 and n_ctx & (n_ctx - 1) == 0 and n_lat % tm == 0 and n_ctx % tm == 0
    pad = lambda t: t.reshape(1, N_RWKV_IN)
    zero = jnp.zeros((DECAY_LORA, W), F32)
    blockdiag = lambda t: jnp.concatenate(
        [jnp.concatenate([t[0], zero], axis=1), jnp.concatenate([zero, t[1]], axis=1)], axis=0).astype(BF16)
    row = lambda i: (i, 0)
    whole = lambda i: (0, 0)
    halo = tm // 8
    full = lambda t: pl.BlockSpec(t.shape, whole)
    params = [pad(p["mu_prev"]), pad(p["mu_next"]), blockdiag(p["w2"]), blockdiag(p["a2"]),
              p["g2"].astype(BF16), p["w0"].reshape(1, N_DIR * W), p["a0"].reshape(1, N_DIR * W),
              p["k_k"].reshape(1, W), p["k_a"].reshape(1, W), p["r_k"].reshape(1, W), hs]
    one = pl.BlockSpec((tm, W), row)
    two = pl.BlockSpec((N_DIR, tm, W), lambda i: (0, i, 0))
    s1 = jax.ShapeDtypeStruct((M, W), F32)
    sb = jax.ShapeDtypeStruct((M, W), BF16)
    s2 = jax.ShapeDtypeStruct((N_DIR, M, W), F32)
    return pl.pallas_call(
        functools.partial(_feat_body, n_lat_rows=n_samples * n_lat, seg_lat=n_lat, seg_ctx=n_ctx),
        grid=(M // tm,),
        in_specs=[pl.BlockSpec((tm, ZW), row),
                  pl.BlockSpec((8, ZW), lambda i: (jnp.maximum(i * halo - 1, 0), 0)),
                  pl.BlockSpec((8, ZW), lambda i: (jnp.minimum((i + 1) * halo, M // 8 - 1), 0))]
                 + [full(t) for t in params],
        out_specs=[one, one, one, two, two, two, one, one],
        out_shape=[s1, sb, s1, s2, s2, s2, sb, sb],
        compiler_params=_params("parallel"),
        name="rwkv_features",
    )(zr, zr, zr, *params)


def _ffn_body(x_ref, mod_ref, g_ref, wg_ref, wu_ref, wo_ref, out_ref, h_ref, acc_ref, *, n_chunks):
    c = pl.program_id(1)

    @pl.when(c == 0)
    def _():
        h_ref[...] = _norm_mod(x_ref[...], g_ref[...], mod_ref[3:4, :], mod_ref[4:5, :]).astype(BF16)
        acc_ref[...] = jnp.zeros_like(acc_ref)

    h = h_ref[...]
    gate = _dot(h, wg_ref[...])
    up = _dot(h, wu_ref[...])
    act = gate * _sigmoid(gate) * up
    acc_ref[...] += _dot(act.astype(BF16), wo_ref[...])

    @pl.when(c == n_chunks - 1)
    def _():
        out_ref[...] = x_ref[...] + mod_ref[5:6, :] * acc_ref[...]


def ffn(xg, modtab, gain, w_in, w_out, n_lat):
    M, D = xg.shape
    hidden = w_out.shape[0]
    n_samples = modtab.shape[0] - 1
    tps = n_lat // ROW_TILE
    assert hidden % FFN_CHUNK == 0
    n_chunks = hidden // FFN_CHUNK
    w_in = w_in.astype(BF16)
    w_out = w_out.astype(BF16)
    return pl.pallas_call(
        functools.partial(_ffn_body, n_chunks=n_chunks),
        grid=(M // ROW_TILE, n_chunks),
        in_specs=[pl.BlockSpec((ROW_TILE, D), lambda i, c: (i, 0)),
                  pl.BlockSpec((None, 6, D), lambda i, c: (_mod_row(i, tps, n_samples), 0, 0)),
                  pl.BlockSpec((1, D), lambda i, c: (0, 0)),
                  pl.BlockSpec((D, FFN_CHUNK), lambda i, c: (0, c)),
                  pl.BlockSpec((D, FFN_CHUNK), lambda i, c: (0, n_chunks + c)),
                  pl.BlockSpec((FFN_CHUNK, D), lambda i, c: (c, 0))],
        out_specs=pl.BlockSpec((ROW_TILE, D), lambda i, c: (i, 0)),
        out_shape=jax.ShapeDtypeStruct((M, D), F32),
        scratch_shapes=[pltpu.VMEM((ROW_TILE, D), BF16), pltpu.VMEM((ROW_TILE, D), F32)],
        compiler_params=_params("parallel", "arbitrary"),
        name="ffn",
    )(xg, modtab, gain.reshape(1, D), w_in, w_in, w_out)


def _dft_body(w_ref, z_ref, o_ref, acc_ref, *, nk, scale):
    k = pl.program_id(2)

    @pl.when(k == 0)
    def _():
        acc_ref[...] = jnp.zeros_like(acc_ref)

    acc_ref[...] += _dot(w_ref[...], z_ref[...])

    @pl.when(k == nk - 1)
    def _():
        o_ref[...] = (acc_ref[...] * scale).astype(o_ref.dtype)


def time_dft(w, z, n_samples, row0, scale, tm=1024, tk=2048):
    n = w.shape[0]
    Wd = z.shape[1] // 2
    tm, tk = _pick(n, tm), _pick(n, tk)
    nkh = n // tk
    nk = 2 * nkh
    assert row0 % tk == 0
    off = row0 // tk
    return pl.pallas_call(
        functools.partial(_dft_body, nk=nk, scale=scale),
        grid=(n_samples, n // tm, nk),
        in_specs=[pl.BlockSpec((tm, tk), lambda b, i, k: (i, k)),
                  pl.BlockSpec((tk, Wd), lambda b, i, k: (off + b * nkh + k % nkh, k // nkh))],
        out_specs=pl.BlockSpec((tm, Wd), lambda b, i, k: (b * (n // tm) + i, 0)),
        out_shape=jax.ShapeDtypeStruct((n_samples * n, Wd), BF16),
        scratch_shapes=[pltpu.VMEM((tm, Wd), F32)],
        compiler_params=_params("parallel", "parallel", "arbitrary"),
        name="time_dft",
    )(w, z)


def _delta_chains(chains, m_a, bdmask):
    C = CHUNK
    bf = lambda t: t.astype(BF16)
    zero = jnp.zeros((), BF16)
    cat = lambda ts: jnp.concatenate(ts, axis=0)

    def blk(t):
        t = bf(t)
        return cat([jnp.where(m_a, t, zero), jnp.where(m_a, zero, t)])

    a4 = [_dot_nt(cat([bf(c["ah"]), bf(c["rh"])]), cat([blk(c["bh"]), blk(c["kh"])])) for c in chains]
    a_ab = [jnp.where(c["strict2"], a[:C, :2 * C], 0.0) for c, a in zip(chains, a4)]
    a_ak = [jnp.where(c["strict2"], a[:C, 2 * C:], 0.0) for c, a in zip(chains, a4)]
    a_r = [jnp.where(c["incl4"], a[C:], 0.0) for c, a in zip(chains, a4)]
    x = a_ab
    pw = a_ab
    pwb = [blk(t) for t in pw]
    for _ in range(int(math.log2(C)) - 1):
        pw = [_dot(bf(t), tb) for t, tb in zip(pw, pwb)]
        pwb = [blk(t) for t in pw]
        x = [xi + t + _dot(bf(xi), tb) for xi, t, tb in zip(x, pw, pwb)]
    spb = [bf(c["sp"]) for c in chains]
    vb = [blk(c["v"]) for c in chains]
    x1 = [_dot_nt(bf(c["ah"]), s) + _dot(bf(a), v) for c, s, a, v in zip(chains, spb, a_ak, vb)]
    u = [t + _dot(bf(xi), blk(t)) for t, xi in zip(x1, x)]
    y = [_dot_nt(bf(c["rh"]), s) + _dot(bf(a), cat([blk(ui), v]))
         for c, s, a, ui, v in zip(chains, spb, a_r, u, vb)]
    upd = [_dot_tn(cat([bf(ui), bf(c["v"])]), cat([bf(c["bb"]), bf(c["kb"])])) for c, ui in zip(chains, u)]
    sp_new = [c["sp"] * c["dtot"] + jnp.where(bdmask, t, 0.0) for c, t in zip(chains, upd)]
    return list(zip(y, sp_new))


def _split3(t):
    hi = t.astype(BF16)
    r1 = t - hi.astype(F32)
    mid = r1.astype(BF16)
    lo = (r1 - mid.astype(F32)).astype(BF16)
    return hi, mid, lo


def _delta_prep(r, lw, k, v, kk, b, rev):
    C = CHUNK

    def causal(width, strict):
        ti = lax.broadcasted_iota(jnp.int32, (C, width), 0)
        ii = lax.broadcasted_iota(jnp.int32, (C, width), 1) & (C - 1)
        if rev:
            return ii > ti if strict else ii >= ti
        return ii < ti if strict else ii <= ti

    tri = causal(C, False).astype(BF16)
    L = sum(_dot(tri, part) for part in _split3(lw))
    last = 0 if rev else C - 1
    ltot = L[last:last + 1, :]
    e_nl = jnp.exp(-L)
    e_t = jnp.exp(ltot - L)
    return dict(ah=-kk * jnp.exp(L - lw), rh=r * jnp.exp(L), bh=b * e_nl, kh=k * e_nl, bb=b * e_t,
                kb=k * e_t, v=v, dtot=jnp.exp(ltot)), causal(2 * C, True), causal(4 * C, False)


def _delta_body(*refs, nb):
    ins, outs, s_ref = refs[:12 * nb], refs[12 * nb:14 * nb], refs[14 * nb]

    @pl.when(pl.program_id(1) == 0)
    def _():
        s_ref[...] = jnp.zeros_like(s_ref)

    C, W = ins[0].shape
    m_a = lax.broadcasted_iota(jnp.int32, (C, LANES), 1) < (LANES // 2)
    br = lax.broadcasted_iota(jnp.int32, (LANES, LANES), 0) < (LANES // 2)
    bc = lax.broadcasted_iota(jnp.int32, (LANES, LANES), 1) < (LANES // 2)
    bdmask = br == bc
    chains, dest = [], []
    for d in range(N_DIR):
        for j in range(nb):
            r_ref, v_ref, kk_ref, lw_ref, k_ref, b_ref = ins[(d * nb + j) * 6:(d * nb + j + 1) * 6]
            ops, strict2, incl4 = _delta_prep(r_ref[...], lw_ref[...], k_ref[...], v_ref[...], kk_ref[...],
                                              b_ref[...], rev=(d == 1))
            for p in range(W // LANES):
                sl = slice(p * LANES, (p + 1) * LANES)
                c = {name: t[:, sl] for name, t in ops.items()}
                c.update(sp=s_ref[d, j, p], strict2=strict2, incl4=incl4)
                chains.append(c)
                dest.append((outs[d * nb + j], d, j, p, sl))
    for (y, sp_new), (y_ref, d, j, p, sl) in zip(_delta_chains(chains, m_a, bdmask), dest):
        y_ref[:, sl] = y
        s_ref[d, j, p] = sp_new


def rwkv_scan(r, v, kk, lw, k, b, n_samples, n_lat, n_ctx, samples=2):
    M, W = r.shape
    ncl, ncc = n_lat // CHUNK, n_ctx // CHUNK
    nc = ncl + ncc
    nb = _pick(n_samples, samples)
    ng = n_samples // nb

    def row_block(n_s, s, t):
        return jnp.where(t < ncc, n_s * ncl + s * ncc + t, s * ncl + (t - ncc))

    def step_chunk(d, c):
        if d == 0:
            return c
        return jnp.where(c < ncc, ncc - 1 - c, nc + ncc - 1 - c)

    in_specs, out_specs, args = [], [], []
    for d in range(N_DIR):
        for j in range(nb):
            def rows(bi, c, d=d, j=j):
                return row_block(n_samples, j * ng + bi, step_chunk(d, c))

            def out_rows(bi, c, d=d):
                return row_block(ng, bi, step_chunk(d, c))
            shared = pl.BlockSpec((CHUNK, W), lambda bi, c, rows=rows: (rows(bi, c), 0))
            per_dir = pl.BlockSpec((None, CHUNK, W), lambda bi, c, rows=rows, d=d: (d, rows(bi, c), 0))
            in_specs += [shared] * 3 + [per_dir] * 3
            args += [r, v, kk, lw, k, b]
            out_specs.append(pl.BlockSpec((CHUNK, W), lambda bi, c, out_rows=out_rows: (out_rows(bi, c), 0)))
    outs = pl.pallas_call(
        functools.partial(_delta_body, nb=nb),
        grid=(ng, nc),
        in_specs=in_specs,
        out_specs=out_specs,
        out_shape=[jax.ShapeDtypeStruct((M // nb, W), F32)] * (N_DIR * nb),
        scratch_shapes=[pltpu.VMEM((N_DIR, nb, W // LANES, LANES, LANES), F32)],
        compiler_params=_params("parallel", "arbitrary"),
        name="delta_scan",
    )(*args)
    return outs


VT_ONES = 16
ATTN_LOOKAHEAD = 6


def _attn_body(*refs, n_seg, kv_chunk):
    q_ref, kv, o_ref = refs[0], refs[1:1 + 2 * n_seg], refs[1 + 2 * n_seg]
    items = []
    for g in range(n_seg):
        n = kv[2 * g].shape[0]
        ck = min(kv_chunk, n)
        items += [(hh, g, c0, ck) for c0 in range(0, n, ck) for hh in range(2)]
    q = [q_ref[:, hh * QK_PAD:(hh + 1) * QK_PAD] for hh in range(2)]
    m = [None, None]
    acc = [None, None]
    scores = {}
    for idx in range(len(items) + ATTN_LOOKAHEAD):
        if idx < len(items):
            hh, g, c0, ck = items[idx]
            scores[idx] = _dot_nt(kv[2 * g][c0:c0 + ck, hh * QK_PAD:(hh + 1) * QK_PAD], q[hh])
        if idx >= ATTN_LOOKAHEAD:
            hh, g, c0, ck = items[idx - ATTN_LOOKAHEAD]
            st = scores.pop(idx - ATTN_LOOKAHEAD)
            mc = jnp.max(st, axis=0, keepdims=True)
            m_new = mc if m[hh] is None else jnp.maximum(m[hh], mc)
            pt = jnp.exp2(st - m_new).astype(BF16)
            v0 = hh * QK_PAD + QK_PAD - V_HEAD - VT_ONES
            pv = _dot(kv[2 * g + 1][v0:v0 + VT_ONES + V_HEAD, c0:c0 + ck], pt)
            acc[hh] = pv if m[hh] is None else acc[hh] * jnp.exp2(m[hh] - m_new) + pv
            m[hh] = m_new
    outs = [a[VT_ONES:] / a[0:1] for a in acc]
    o_ref[...] = jnp.concatenate(outs, axis=0).T.astype(o_ref.dtype)


def attention(q, k, vt, n_samples, q_row0, n_q, segments, tq=256, kv_chunk=256):
    H = q.shape[1] // QK_PAD
    tq = _pick(n_q, tq)
    assert q_row0 % tq == 0
    nq = n_q // tq
    in_specs = [pl.BlockSpec((tq, 2 * QK_PAD), lambda b, h, i: (q_row0 // tq + b * nq + i, h))]
    args = [q]
    for row0, n in segments:
        assert row0 % n == 0
        in_specs += [pl.BlockSpec((n, 2 * QK_PAD), lambda b, h, i, row0=row0, n=n: (row0 // n + b, h)),
                     pl.BlockSpec((2 * QK_PAD, n), lambda b, h, i, row0=row0, n=n: (h, row0 // n + b))]
        args += [k, vt]
    return pl.pallas_call(
        functools.partial(_attn_body, n_seg=len(segments), kv_chunk=kv_chunk),
        grid=(n_samples, H // 2, nq),
        in_specs=in_specs,
        out_specs=pl.BlockSpec((tq, 2 * V_HEAD), lambda b, h, i: (b * nq + i, h)),
        out_shape=jax.ShapeDtypeStruct((n_samples * n_q, H * V_HEAD), BF16),
        compiler_params=_params("parallel", "parallel", "parallel"),
        name="attention",
    )(*args)


def _lane_sum(t, ones):
    tb = t.astype(BF16)
    return jnp.concatenate([_dot(tb[:, c:c + LANES], ones) for c in range(0, t.shape[1], LANES)], axis=1)


def _mla_body(z_ref, gcq_ref, gckv_ref, wq_ref, wk_ref, wv_ref, gq_ref, gk_ref, cos_ref, sin_ref, ones_ref,
              q_ref, k_ref, vt_ref, *, q_scale):
    tm = z_ref.shape[0]
    H = MLA_HEADS
    ones = ones_ref[...]
    c_q = z_ref[:, :Q_LORA]
    c_kv = z_ref[:, Q_LORA:Q_LORA + KV_LORA]
    cq_n = c_q * lax.rsqrt(jnp.mean(c_q * c_q, axis=-1, keepdims=True) + RMS_EPS) * gcq_ref[...]
    ckv_n = (c_kv * lax.rsqrt(jnp.mean(c_kv * c_kv, axis=-1, keepdims=True) + RMS_EPS)
             * gckv_ref[...]).astype(BF16)
    lane = lax.broadcasted_iota(jnp.int32, (tm, LANES), 1)
    k_rope = pltpu.roll(jnp.where(lane < QK_ROPE, z_ref[:, Q_LORA + KV_LORA:Q_LORA + KV_LORA + LANES], 0.0),
                        QK_NOPE, 1)
    q = _dot(cq_n.astype(BF16), wq_ref[...])
    k = _dot(ckv_n, wk_ref[...]) + jnp.concatenate([k_rope] * H, axis=1)
    cos = jnp.concatenate([cos_ref[...]] * H, axis=1)
    sin = jnp.concatenate([sin_ref[...]] * H, axis=1)
    first_half = (lax.broadcasted_iota(jnp.int32, (tm, H * LANES), 1) & (ROPE_AXIS // 2)) == 0

    def finish(t, gain, scale):
        t = t * lax.rsqrt(_lane_sum(t * t, ones) * (1.0 / QK_HEAD) + RMS_EPS) * gain
        partner = jnp.where(first_half, pltpu.roll(t, H * LANES - ROPE_AXIS // 2, 1),
                            pltpu.roll(t, ROPE_AXIS // 2, 1))
        return ((t * cos + partner * sin) * scale).astype(BF16)

    q_ref[...] = finish(q, gq_ref[...], q_scale)
    k_ref[...] = finish(k, gk_ref[...], 1.0)
    v = _dot(ckv_n, wv_ref[...])
    ones_rows = (lax.broadcasted_iota(jnp.int32, (tm, H * LANES), 1) & (LANES - 1)) < QK_PAD - V_HEAD
    vt_ref[...] = jnp.where(ones_rows, 1.0, v).T.astype(BF16)


def mla_prep(zm, p, rope_cos, rope_sin, n_lat_rows, n_lat, tm=256):
    M, ZW = zm.shape
    H = MLA_HEADS
    src = jnp.arange(H * QK_HEAD, dtype=jnp.int32)[:, None]
    dst = jnp.arange(H * QK_PAD, dtype=jnp.int32)[None, :]
    spread = ((src // QK_HEAD == dst // QK_PAD) & (src % QK_HEAD == dst % QK_PAD)).astype(BF16)
    w_uq = matmul_bf16(p["w_uq"].astype(BF16), spread)
    is_k = (jnp.arange(H * QK_PAD, dtype=jnp.int32) % QK_PAD) < QK_NOPE
    w_uk = jnp.where(is_k, p["w_ukv"], 0.0).astype(BF16)
    w_uv = jnp.where(is_k, 0.0, p["w_ukv"]).astype(BF16)
    gain = lambda g: jnp.tile(jnp.pad(g, (0, QK_PAD - QK_HEAD)), H).reshape(1, H * QK_PAD)
    ones = jnp.ones((LANES, LANES), BF16)
    consts = [p["g_cq"].reshape(1, Q_LORA), p["g_ckv"].reshape(1, KV_LORA), w_uq, w_uk, w_uv,
              gain(p["g_qn"]), gain(p["g_kn"])]
    row = lambda i: (i, 0)
    whole = lambda i: (0, 0)
    full = lambda t: pl.BlockSpec(t.shape, whole)
    lat_tiles, per_sample = n_lat_rows // tm, n_lat // tm
    tab = lambda i: (jnp.where(i < lat_tiles, i % per_sample, per_sample), 0)
    return pl.pallas_call(
        functools.partial(_mla_body, q_scale=QK_HEAD ** -0.5 * math.log2(math.e)),
        grid=(M // tm,),
        in_specs=[pl.BlockSpec((tm, ZW), row)] + [full(t) for t in consts]
                 + [pl.BlockSpec((tm, LANES), tab), pl.BlockSpec((tm, LANES), tab), full(ones)],
        out_specs=[pl.BlockSpec((tm, H * QK_PAD), row), pl.BlockSpec((tm, H * QK_PAD), row),
                   pl.BlockSpec((H * QK_PAD, tm), lambda i: (0, i))],
        out_shape=[jax.ShapeDtypeStruct((M, H * QK_PAD), BF16), jax.ShapeDtypeStruct((M, H * QK_PAD), BF16),
                   jax.ShapeDtypeStruct((H * QK_PAD, M), BF16)],
        compiler_params=_params("parallel"),
        name="mla_prep",
    )(zm, *consts, rope_cos, rope_sin, ones)


def _rope_lane_tables(n_tokens, tile):
    ang_row, ang_col = _rope_tables(n_tokens)
    one = jnp.ones((n_tokens, QK_NOPE), F32)
    zero = jnp.zeros((n_tokens, QK_NOPE), F32)
    pad1 = jnp.ones((n_tokens, QK_PAD - QK_HEAD), F32)
    pad0 = jnp.zeros((n_tokens, QK_PAD - QK_HEAD), F32)
    cr, cc, sr, sc = jnp.cos(ang_row), jnp.cos(ang_col), jnp.sin(ang_row), jnp.sin(ang_col)
    cos = jnp.concatenate([one, cr, cr, cc, cc, pad1], axis=1)
    sin = jnp.concatenate([zero, -sr, sr, -sc, sc, pad0], axis=1)
    cos = jnp.concatenate([cos, jnp.ones((tile, QK_PAD), F32)], axis=0)
    sin = jnp.concatenate([sin, jnp.zeros((tile, QK_PAD), F32)], axis=0)
    return cos, sin


def _rope_tables(n_tokens):
    rows = n_tokens // GRID_W
    row = jnp.broadcast_to(jnp.arange(rows)[:, None], (rows, GRID_W)).reshape(-1).astype(F32)
    col = jnp.broadcast_to(jnp.arange(GRID_W)[None, :], (rows, GRID_W)).reshape(-1).astype(F32)
    inv_freq = ROPE_THETA ** (-jnp.arange(0, ROPE_AXIS, 2, dtype=F32) / ROPE_AXIS)
    return row[:, None] * inv_freq, col[:, None] * inv_freq


def _dft_time_matrix(n):
    f = int(round(math.sqrt(n)))
    assert f * f == n
    k = jnp.arange(n, dtype=jnp.int32)[:, None]
    t = jnp.arange(f, dtype=jnp.int32)[None, :]
    a = ((k * t) % f).astype(F32) * (2.0 * math.pi / f)
    b = ((k * t) % n).astype(F32) * (2.0 * math.pi / n)
    ca, sa = jnp.cos(a)[:, :, None], jnp.sin(a)[:, :, None]
    cb, sb = jnp.cos(b)[:, None, :], jnp.sin(b)[:, None, :]
    cos = (ca * cb - sa * sb).reshape(n, n)
    sin = (sa * cb + ca * sb).reshape(n, n)
    return jnp.concatenate([cos, -sin], axis=1).astype(BF16)


def _dft_channel_matrix():
    idx = jnp.arange(FNO_GROUP_DIM, dtype=jnp.int32)
    ang = ((idx[:, None] * idx[None, :]) % FNO_GROUP_DIM).astype(F32) * (2.0 * math.pi / FNO_GROUP_DIM)
    eye = jnp.eye(FNO_GROUPS, dtype=F32)
    return jnp.concatenate([jnp.kron(eye, jnp.cos(ang)), jnp.kron(eye, jnp.sin(ang))], axis=1)


def _layer(xg, c_act, cctx_act, p, last, consts, dims):
    B, S, Sc = dims
    M, D = xg.shape
    ML = B * S
    dft_x, dft_c, dft_ch, (rope_cos, rope_sin), head_ones = consts

    act = jnp.concatenate([c_act, cctx_act[None, :], jnp.zeros((7 - B, D), F32)], axis=0)
    modtab = matmul_f32_bias(act, p["w_mod"], p["b_mod"])[:B + 1].reshape(B + 1, 6, D)

    w_in = p["w_in"]
    o1 = N_RWKV_IN
    o2 = o1 + N_MLA_IN
    o3 = o2 + FNO_WIDTH
    w_fd = matmul_f32_bias(w_in[:, o2:o3], dft_ch, jnp.zeros((2 * FNO_WIDTH,), F32))
    zr, zm, zg, zd = inproj(xg, modtab, p["g_norm1"], w_in[:, :ZR_PAD].astype(BF16),
                            w_in[:, o1:o1 + ZM_PAD].astype(BF16), w_in[:, o3:].astype(BF16),
                            w_fd.astype(BF16), S)

    r, v, kkf, lw_dir, k_dir, b_dir, g_out, bonus = rwkv_features(zr, p, head_ones, B, S, Sc)
    y_slots = rwkv_scan(r, v, kkf, lw_dir, k_dir, b_dir, B, S, Sc)

    qh, kh, vh = mla_prep(zm, p, rope_cos, rope_sin, ML, S)
    att = attention(qh, kh, vh, B, 0, S, [(0, S), (ML, Sc)])

    yf = time_dft(dft_x, zd, B, 0, 1.0 / math.sqrt(S * FNO_GROUP_DIM))

    n_rows = ML if last else M
    if not last:
        att = jnp.concatenate([att, attention(qh, kh, vh, B, ML, Sc, [(ML, Sc)])], axis=0)
        yf = jnp.concatenate([yf, time_dft(dft_c, zd, B, ML, 1.0 / math.sqrt(Sc * FNO_GROUP_DIM))], axis=0)
    x_mid = post(y_slots, bonus, g_out, p["gn_w"], p["gn_b"], head_ones, att, yf, zg, xg, modtab,
                 p["w_rwkv_o"], p["w_mla_o"], p["w_fno"], p["w_out"], n_rows, S, Sc)
    return ffn(x_mid, modtab, p["g_norm2"], p["w_ffn_in"], p["w_ffn_out"], S)


def kernel(x, c, ctx, c_ctx, w_mod, b_mod, g_norm1, w_in, mu_prev, mu_next, w0, w2, a0, a2, k_k, k_a, r_k, g2, gn_w, gn_b, w_rwkv_o, g_cq, g_ckv, w_uq, w_ukv, g_qn, g_kn, w_mla_o, w_fno, w_out, g_norm2, w_ffn_in, w_ffn_out):
    layer_params = dict(w_mod=w_mod, b_mod=b_mod, g_norm1=g_norm1, w_in=w_in, mu_prev=mu_prev,
                        mu_next=mu_next, w0=w0, w2=w2, a0=a0, a2=a2, k_k=k_k, k_a=k_a, r_k=r_k,
                        g2=g2, gn_w=gn_w, gn_b=gn_b, w_rwkv_o=w_rwkv_o, g_cq=g_cq, g_ckv=g_ckv,
                        w_uq=w_uq, w_ukv=w_ukv, g_qn=g_qn, g_kn=g_kn, w_mla_o=w_mla_o, w_fno=w_fno,
                        w_out=w_out, g_norm2=g_norm2, w_ffn_in=w_ffn_in, w_ffn_out=w_ffn_out)
    depth = w_mod.shape[0]
    B, S, D = x.shape
    Sc = ctx.shape[1]
    assert S % ROW_TILE == 0 and (B * Sc) % ROW_TILE == 0 and B <= 7
    head_ones = jnp.kron(jnp.eye(RWKV_HEADS, dtype=F32),
                         jnp.ones((RWKV_HEAD_DIM, RWKV_HEAD_DIM), F32)).astype(BF16)
    consts = (_dft_time_matrix(S), _dft_time_matrix(Sc), _dft_channel_matrix(),
              _rope_lane_tables(S, 256), head_ones)
    c_act = jax.nn.silu(c)
    cctx_act = jax.nn.silu(c_ctx)
    xg = jnp.concatenate([x.reshape(B * S, D), ctx.reshape(B * Sc, D)], axis=0)
    for layer in range(depth):
        p = {name: arr[layer] for name, arr in layer_params.items()}
        xg = _layer(xg, c_act, cctx_act, p, layer == depth - 1, consts, (B, S, Sc))
    return xg[:B * S].reshape(B, S, D)
```

```python
import functools
import math

import jax
import jax.numpy as jnp
from jax import lax
from jax.experimental import pallas as pl
from jax.experimental.pallas import tpu as pltpu

F32 = jnp.float32
BF16 = jnp.bfloat16

D_MODEL = 1024
GRID_W = 64
RWKV_HEADS = 8
RWKV_HEAD_DIM = 64
RWKV_WIDTH = RWKV_HEADS * RWKV_HEAD_DIM
DECAY_LORA = 64
AAA_LORA = 64
GATE_LORA = 128
N_DIR = 2
GN_EPS = 64e-5
MLA_HEADS = 8
QK_NOPE = 64
QK_ROPE = 32
QK_HEAD = QK_NOPE + QK_ROPE
V_HEAD = 64
Q_LORA = 384
KV_LORA = 128
ROPE_AXIS = QK_ROPE // 2
ROPE_THETA = 10000.0
FNO_GROUPS = 8
FNO_GROUP_DIM = 64
FNO_WIDTH = FNO_GROUPS * FNO_GROUP_DIM
N_RWKV_IN = 3 * RWKV_WIDTH + N_DIR * DECAY_LORA + N_DIR * AAA_LORA + GATE_LORA
N_MLA_IN = Q_LORA + KV_LORA + QK_ROPE
RMS_EPS = 1e-6

LANES = 128
CHUNK = 64
SCAN_PRECISE_LEVELS = 3
QK_PAD = LANES
ROW_TILE = 1024
ZR_PAD = 2048
ZM_PAD = 1024
COL_TILE = 1024
FFN_CHUNK = 256
VMEM_LIMIT = 48 * 1024 * 1024


def _params(*sem):
    return pltpu.CompilerParams(dimension_semantics=sem, vmem_limit_bytes=VMEM_LIMIT)


def _dot(a, b):
    return jnp.dot(a, b, preferred_element_type=F32)


def _dot_nt(a, b):
    return lax.dot_general(a, b, (((1,), (1,)), ((), ())), preferred_element_type=F32)


def _dot_tn(a, b):
    return lax.dot_general(a, b, (((0,), (0,)), ((), ())), preferred_element_type=F32)


def _sigmoid(x):
    return 1.0 / (1.0 + jnp.exp(-x))


def _pick(dim, pref):
    t = min(dim, pref)
    while dim % t:
        t //= 2
    return t


def _mm_f32_body(a_ref, b_ref, bias_ref, o_ref):
    o_ref[...] = jnp.dot(a_ref[...], b_ref[...], preferred_element_type=F32,
                         precision=lax.Precision.HIGHEST) + bias_ref[...]


def _mm_bf16_body(a_ref, b_ref, o_ref):
    o_ref[...] = _dot(a_ref[...], b_ref[...]).astype(o_ref.dtype)


def matmul_bf16(a, b):
    return pl.pallas_call(
        _mm_bf16_body,
        out_shape=jax.ShapeDtypeStruct((a.shape[0], b.shape[1]), BF16),
        name="mm_bf16",
    )(a, b)


def matmul_f32_bias(a, b, bias, tn=512):
    M, K = a.shape
    _, N = b.shape
    return pl.pallas_call(
        _mm_f32_body,
        grid=(N // tn,),
        in_specs=[pl.BlockSpec((M, K), lambda j: (0, 0)),
                  pl.BlockSpec((K, tn), lambda j: (0, j)),
                  pl.BlockSpec((1, tn), lambda j: (0, j))],
        out_specs=pl.BlockSpec((M, tn), lambda j: (0, j)),
        out_shape=jax.ShapeDtypeStruct((M, N), F32),
        compiler_params=_params("parallel"),
        name="mm_f32_bias",
    )(a, b, bias.reshape(1, N))


def _norm_mod(x, gain, shift, scale):
    y = x * lax.rsqrt(jnp.mean(x * x, axis=-1, keepdims=True) + RMS_EPS)
    return (y * gain) * (1.0 + scale) + shift


def _inproj_body(x_ref, mod_ref, g_ref, wr_ref, wm_ref, wg_ref, wd_ref, zr_ref, zm_ref, zg_ref, zd_ref,
                 h_ref):
    j = pl.program_id(1)

    @pl.when(j == 0)
    def _():
        h_ref[...] = _norm_mod(x_ref[...], g_ref[...], mod_ref[0:1, :], mod_ref[1:2, :]).astype(BF16)

    @pl.when(j < 2)
    def _():
        zr_ref[...] = _dot(h_ref[...], wr_ref[...])

    @pl.when(j == 2)
    def _():
        zm_ref[...] = _dot(h_ref[...], wm_ref[...])

    @pl.when((j >= 3) & (j < 6))
    def _():
        zg_ref[...] = _sigmoid(_dot(h_ref[...], wg_ref[...])).astype(BF16)

    @pl.when(j == 6)
    def _():
        zd_ref[...] = _dot(h_ref[...], wd_ref[...]).astype(BF16)


def _mod_row(i, tiles_per_sample, n_samples):
    return jnp.minimum(i // tiles_per_sample, n_samples)


def inproj(xg, modtab, gain, w_r, w_m, w_g, w_d, n_lat):
    M, D = xg.shape
    n_samples = modtab.shape[0] - 1
    tps = n_lat // ROW_TILE
    clamp = lambda j, lo, hi: jnp.minimum(jnp.maximum(j, lo), hi)
    zr_col = lambda i, j: clamp(j, 0, 1)
    zg_col = lambda i, j: clamp(j - 3, 0, 2)
    first = lambda i, j: 0
    wtile = lambda col: pl.BlockSpec((D, COL_TILE), lambda i, j: (0, col(i, j)))
    otile = lambda col: pl.BlockSpec((ROW_TILE, COL_TILE), lambda i, j: (i, col(i, j)))
    return pl.pallas_call(
        _inproj_body,
        grid=(M // ROW_TILE, 7),
        in_specs=[pl.BlockSpec((ROW_TILE, D), lambda i, j: (i, 0)),
                  pl.BlockSpec((None, 6, D), lambda i, j: (_mod_row(i, tps, n_samples), 0, 0)),
                  pl.BlockSpec((1, D), lambda i, j: (0, 0)),
                  wtile(zr_col), wtile(first), wtile(zg_col), wtile(first)],
        out_specs=[otile(zr_col), otile(first), otile(zg_col), otile(first)],
        out_shape=[jax.ShapeDtypeStruct((M, ZR_PAD), F32),
                   jax.ShapeDtypeStruct((M, ZM_PAD), F32),
                   jax.ShapeDtypeStruct((M, 3 * D), BF16),
                   jax.ShapeDtypeStruct((M, 2 * FNO_WIDTH), BF16)],
        scratch_shapes=[pltpu.VMEM((ROW_TILE, D), BF16)],
        compiler_params=_params("parallel", "arbitrary"),
        name="inproj",
    )(xg, modtab, gain.reshape(1, D), w_r, w_m, w_g, w_d)


def _head_sum(t, hs, terms=1):
    parts = _split3(t)[:terms]
    return sum(_dot(part, hs) for part in parts)


def _slot_tile(i, tm, n_lat, n_ctx, n_samples, nb):
    ng = n_samples // nb
    tps, lat_tiles = n_lat // tm, n_samples * n_lat // tm
    ctx_tiles = ng * n_ctx // tm
    sample = i // tps
    c = i - lat_tiles
    slot = jnp.where(i < lat_tiles, sample // ng, c // ctx_tiles)
    local = jnp.where(i < lat_tiles, (sample % ng) * tps + i % tps, ng * tps + c % ctx_tiles)
    return slot, local


def _post_body(*refs, nb, slot_of, lat_tiles):
    y_refs = refs[:2 * nb]
    (bonus_ref, gate_ref, gnw_ref, gnb_ref, hs_ref, att_l_ref, att_c_ref, four_l_ref, four_c_ref, zg_ref,
     x_ref, mod_ref, wr_ref, wm_ref, wf_ref, wo_ref, out_ref) = refs[2 * nb:]
    D = x_ref.shape[1]
    is_ctx = pl.program_id(0) >= lat_tiles
    att = jnp.where(is_ctx, att_c_ref[...], att_l_ref[...])
    four = jnp.where(is_ctx, four_c_ref[...], four_l_ref[...])
    slot = slot_of(pl.program_id(0))
    y = y_refs[0][...] + y_refs[nb][...]
    for j in range(1, nb):
        y = jnp.where(slot == j, y_refs[j][...] + y_refs[nb + j][...], y)
    hs = hs_ref[...]
    inv_n = 1.0 / RWKV_HEAD_DIM
    dev = y - _head_sum(y, hs, terms=2) * inv_n
    var = _head_sum(dev * dev, hs) * inv_n
    yn = dev * lax.rsqrt(var + GN_EPS) * gnw_ref[...] + gnb_ref[...]
    o = ((yn + bonus_ref[...]) * gate_ref[...]).astype(BF16)
    a = _dot(o, wr_ref[...])
    b = _dot(att, wm_ref[...])
    f = _dot(four, wf_ref[...])
    merged = (zg_ref[:, 0:D].astype(F32) * a + zg_ref[:, D:2 * D].astype(F32) * b
              + zg_ref[:, 2 * D:3 * D].astype(F32) * f)
    out_ref[...] = x_ref[...] + mod_ref[2:3, :] * _dot(merged.astype(BF16), wo_ref[...])


def post(y_slots, bonus, gate, gn_w, gn_b, hs, att, four, zg, xg, modtab, w_r, w_m, w_f, w_o,
         n_rows, n_lat, n_ctx, tm=512):
    D = xg.shape[1]
    W = bonus.shape[1]
    n_samples = modtab.shape[0] - 1
    nb = len(y_slots) // N_DIR
    assert (n_samples // nb * n_ctx) % tm == 0 and n_lat % tm == 0
    tps = n_lat // tm
    lat_tiles = n_samples * tps
    lat_rows = lambda width: pl.BlockSpec((tm, width), lambda i: (jnp.minimum(i, lat_tiles - 1), 0))
    ctx_rows = lambda width: pl.BlockSpec((tm, width), lambda i: (jnp.maximum(i - lat_tiles, 0), 0))
    slot_tile = functools.partial(_slot_tile, tm=tm, n_lat=n_lat, n_ctx=n_ctx, n_samples=n_samples, nb=nb)
    row = lambda i: (i, 0)
    whole = lambda i: (0, 0)
    rows = lambda width: pl.BlockSpec((tm, width), row)
    wspec = pl.BlockSpec((W, D), whole)
    vec = pl.BlockSpec((1, W), whole)

    def slot_spec(j):
        def index(i):
            slot, local = slot_tile(i)
            return (jnp.where(slot == j, local, 0), 0)
        return pl.BlockSpec((tm, W), index)

    return pl.pallas_call(
        functools.partial(_post_body, nb=nb, slot_of=lambda i: slot_tile(i)[0], lat_tiles=lat_tiles),
        grid=(n_rows // tm,),
        in_specs=[slot_spec(j) for _ in range(N_DIR) for j in range(nb)]
                 + [rows(W), rows(W), vec, vec, pl.BlockSpec((W, W), whole),
                    lat_rows(att[0].shape[1]), ctx_rows(att[1].shape[1]),
                    lat_rows(four[0].shape[1]), ctx_rows(four[1].shape[1]), rows(3 * D), rows(D),
                    pl.BlockSpec((None, 6, D), lambda i: (_mod_row(i, tps, n_samples), 0, 0)),
                    wspec, wspec, wspec, pl.BlockSpec((D, D), whole)],
        out_specs=rows(D),
        out_shape=jax.ShapeDtypeStruct((n_rows, D), F32),
        compiler_params=_params("parallel"),
        name="post",
    )(*y_slots, bonus, gate, gn_w.reshape(1, W), gn_b.reshape(1, W), hs, *att, *four, zg, xg, modtab,
      w_r.astype(BF16), w_m.astype(BF16), w_f.astype(BF16), w_o.astype(BF16))


def _feat_body(z_ref, zp_ref, zn_ref, mup_ref, mun_ref, w2_ref, a2_ref, g2_ref, w0_ref, a0_ref, kk_ref,
               ka_ref, rk_ref, hs_ref, r_ref, v_ref, kkn_ref, lw_ref, kd_ref, bd_ref, gate_ref, bonus_ref,
               *, n_lat_rows, seg_lat, seg_ctx):
    tm = z_ref.shape[0]
    nz = N_RWKV_IN
    W = RWKV_WIDTH
    i = pl.program_id(0)
    z = z_ref[:, :nz]
    loc = lax.broadcasted_iota(jnp.int32, (tm, 1), 0)
    row = loc + i * tm
    is_lat = row < n_lat_rows
    pos = jnp.where(is_lat, row & (seg_lat - 1), (row - n_lat_rows) & (seg_ctx - 1))
    seg_last = jnp.where(is_lat, seg_lat - 1, seg_ctx - 1)
    z_prev = jnp.where(loc == 0, zp_ref[7:8, :nz], pltpu.roll(z, 1, 0))
    z_prev = jnp.where(pos == 0, 0.0, z_prev)
    z_next = jnp.where(loc == tm - 1, zn_ref[0:1, :nz], pltpu.roll(z, tm - 1, 0))
    z_next = jnp.where(pos == seg_last, 0.0, z_next)
    z = z + (z_prev - z) * mup_ref[...] + (z_next - z) * mun_ref[...]

    r, k, v = z[:, :W], z[:, W:2 * W], z[:, 2 * W:3 * W]
    zw = jnp.tanh(z[:, 3 * W:3 * W + LANES]).astype(BF16)
    za = z[:, 3 * W + LANES:3 * W + 2 * LANES].astype(BF16)
    zg = _sigmoid(z[:, 3 * W + 2 * LANES:3 * W + 3 * LANES]).astype(BF16)
    lw = (-math.exp(-0.5)) * _sigmoid(_dot(zw, w2_ref[...]) + w0_ref[...])
    a = _sigmoid(_dot(za, a2_ref[...]) + a0_ref[...])
    hs = hs_ref[...]
    kq = k * kk_ref[...]
    kkn = kq * lax.rsqrt(jnp.maximum(_head_sum(kq * kq, hs), 1e-24))
    k_sum = None
    for d in range(N_DIR):
        a_d = a[:, d * W:(d + 1) * W]
        k_d = k * (1.0 + (a_d - 1.0) * ka_ref[...])
        lw_ref[d] = lw[:, d * W:(d + 1) * W]
        kd_ref[d] = k_d
        bd_ref[d] = kkn * a_d
        k_sum = k_d if k_sum is None else k_sum + k_d
    r_ref[...] = r
    v_ref[...] = v.astype(v_ref.dtype)
    kkn_ref[...] = kkn
    gate_ref[...] = _dot(zg, g2_ref[...]).astype(gate_ref.dtype)
    bonus_ref[...] = (_head_sum(r * rk_ref[...] * k_sum, hs) * v).astype(bonus_ref.dtype)


def rwkv_features(zr, p, hs, n_samples, n_lat, n_ctx, tm=256):
    M, ZW = zr.shape
    W = RWKV_WIDTH
    assert n_lat & (n_lat - 1) == 0 and n_ctx & (n_ctx - 1) == 0 and n_lat % tm == 0 and n_ctx % tm == 0
    pad = lambda t: t.reshape(1, N_RWKV_IN)
    zero = jnp.zeros((DECAY_LORA, W), F32)
    blockdiag = lambda t: jnp.concatenate(
        [jnp.concatenate([t[0], zero], axis=1), jnp.concatenate([zero, t[1]], axis=1)], axis=0).astype(BF16)
    row = lambda i: (i, 0)
    whole = lambda i: (0, 0)
    halo = tm // 8
    full = lambda t: pl.BlockSpec(t.shape, whole)
    params = [pad(p["mu_prev"]), pad(p["mu_next"]), blockdiag(p["w2"]), blockdiag(p["a2"]),
              p["g2"].astype(BF16), p["w0"].reshape(1, N_DIR * W), p["a0"].reshape(1, N_DIR * W),
              p["k_k"].reshape(1, W), p["k_a"].reshape(1, W), p["r_k"].reshape(1, W), hs]
    one = pl.BlockSpec((tm, W), row)
    two = pl.BlockSpec((N_DIR, tm, W), lambda i: (0, i, 0))
    s1 = jax.ShapeDtypeStruct((M, W), F32)
    sb = jax.ShapeDtypeStruct((M, W), BF16)
    s2 = jax.ShapeDtypeStruct((N_DIR, M, W), F32)
    return pl.pallas_call(
        functools.partial(_feat_body, n_lat_rows=n_samples * n_lat, seg_lat=n_lat, seg_ctx=n_ctx),
        grid=(M // tm,),
        in_specs=[pl.BlockSpec((tm, ZW), row),
                  pl.BlockSpec((8, ZW), lambda i: (jnp.maximum(i * halo - 1, 0), 0)),
                  pl.BlockSpec((8, ZW), lambda i: (jnp.minimum((i + 1) * halo, M // 8 - 1), 0))]
                 + [full(t) for t in params],
        out_specs=[one, one, one, two, two, two, one, one],
        out_shape=[s1, sb, s1, s2, s2, s2, sb, sb],
        compiler_params=_params("parallel"),
        name="rwkv_features",
    )(zr, zr, zr, *params)


def _ffn_body(x_ref, mod_ref, g_ref, wg_ref, wu_ref, wo_ref, out_ref, h_ref, acc_ref, *, n_chunks):
    c = pl.program_id(1)

    @pl.when(c == 0)
    def _():
        h_ref[...] = _norm_mod(x_ref[...], g_ref[...], mod_ref[3:4, :], mod_ref[4:5, :]).astype(BF16)
        acc_ref[...] = jnp.zeros_like(acc_ref)

    h = h_ref[...]
    gate = _dot(h, wg_ref[...])
    up = _dot(h, wu_ref[...])
    act = gate * _sigmoid(gate) * up
    acc_ref[...] += _dot(act.astype(BF16), wo_ref[...])

    @pl.when(c == n_chunks - 1)
    def _():
        out_ref[...] = x_ref[...] + mod_ref[5:6, :] * acc_ref[...]


def ffn(xg, modtab, gain, w_in, w_out, n_lat):
    M, D = xg.shape
    hidden = w_out.shape[0]
    n_samples = modtab.shape[0] - 1
    tps = n_lat // ROW_TILE
    assert hidden % FFN_CHUNK == 0
    n_chunks = hidden // FFN_CHUNK
    w_in = w_in.astype(BF16)
    w_out = w_out.astype(BF16)
    return pl.pallas_call(
        functools.partial(_ffn_body, n_chunks=n_chunks),
        grid=(M // ROW_TILE, n_chunks),
        in_specs=[pl.BlockSpec((ROW_TILE, D), lambda i, c: (i, 0)),
                  pl.BlockSpec((None, 6, D), lambda i, c: (_mod_row(i, tps, n_samples), 0, 0)),
                  pl.BlockSpec((1, D), lambda i, c: (0, 0)),
                  pl.BlockSpec((D, FFN_CHUNK), lambda i, c: (0, c)),
                  pl.BlockSpec((D, FFN_CHUNK), lambda i, c: (0, n_chunks + c)),
                  pl.BlockSpec((FFN_CHUNK, D), lambda i, c: (c, 0))],
        out_specs=pl.BlockSpec((ROW_TILE, D), lambda i, c: (i, 0)),
        out_shape=jax.ShapeDtypeStruct((M, D), F32),
        scratch_shapes=[pltpu.VMEM((ROW_TILE, D), BF16), pltpu.VMEM((ROW_TILE, D), F32)],
        compiler_params=_params("parallel", "arbitrary"),
        name="ffn",
    )(xg, modtab, gain.reshape(1, D), w_in, w_in, w_out)


def _dft_body(w_ref, z_ref, o_ref, acc_ref, *, nk, scale):
    k = pl.program_id(2)

    @pl.when(k == 0)
    def _():
        acc_ref[...] = jnp.zeros_like(acc_ref)

    acc_ref[...] += _dot(w_ref[...], z_ref[...])

    @pl.when(k == nk - 1)
    def _():
        o_ref[...] = (acc_ref[...] * scale).astype(o_ref.dtype)


def time_dft(w, z, n_samples, row0, scale, tm=1024, tk=2048):
    n = w.shape[0]
    Wd = z.shape[1] // 2
    tm, tk = _pick(n, tm), _pick(n, tk)
    nkh = n // tk
    nk = 2 * nkh
    assert row0 % tk == 0
    off = row0 // tk
    return pl.pallas_call(
        functools.partial(_dft_body, nk=nk, scale=scale),
        grid=(n_samples, n // tm, nk),
        in_specs=[pl.BlockSpec((tm, tk), lambda b, i, k: (i, k)),
                  pl.BlockSpec((tk, Wd), lambda b, i, k: (off + b * nkh + k % nkh, k // nkh))],
        out_specs=pl.BlockSpec((tm, Wd), lambda b, i, k: (b * (n // tm) + i, 0)),
        out_shape=jax.ShapeDtypeStruct((n_samples * n, Wd), BF16),
        scratch_shapes=[pltpu.VMEM((tm, Wd), F32)],
        compiler_params=_params("parallel", "parallel", "arbitrary"),
        name="time_dft",
    )(w, z)


def _delta_chains(chains, m_a, bdmask):
    C = CHUNK
    bf = lambda t: t.astype(BF16)
    zero = jnp.zeros((), BF16)
    cat = lambda ts: jnp.concatenate(ts, axis=0)

    def blk(t):
        t = bf(t)
        return cat([jnp.where(m_a, t, zero), jnp.where(m_a, zero, t)])

    a4 = [_dot_nt(cat([bf(c["ah"]), bf(c["rh"])]), cat([blk(c["bh"]), blk(c["kh"])])) for c in chains]
    a_ab = [jnp.where(c["strict2"], a[:C, :2 * C], 0.0) for c, a in zip(chains, a4)]
    a_ak = [jnp.where(c["strict2"], a[:C, 2 * C:], 0.0) for c, a in zip(chains, a4)]
    a_r = [jnp.where(c["incl4"], a[C:], 0.0) for c, a in zip(chains, a4)]
    def split2(t):
        hi = bf(t)
        return hi, bf(t - hi.astype(F32))

    def blk2(t):
        hi, lo = split2(t)
        return blk(hi), blk(lo)

    def dot3(a, b2):
        a_hi, a_lo = split2(a)
        return _dot(a_hi, b2[0]) + (_dot(a_hi, b2[1]) + _dot(a_lo, b2[0]))

    x = a_ab
    pw = a_ab
    pwb = [blk2(t) for t in pw]
    for level in range(int(math.log2(C)) - 1):
        if level < SCAN_PRECISE_LEVELS:
            pw = [dot3(t, tb) for t, tb in zip(pw, pwb)]
            pwb = [blk2(t) for t in pw]
            x = [xi + t + dot3(xi, tb) for xi, t, tb in zip(x, pw, pwb)]
        else:
            pw = [_dot(bf(t), tb[0]) for t, tb in zip(pw, pwb)]
            pwb = [(blk(t),) for t in pw]
            x = [xi + t + _dot(bf(xi), tb[0]) for xi, t, tb in zip(x, pw, pwb)]
    spb = [bf(c["sp"]) for c in chains]
    vb = [blk(c["v"]) for c in chains]
    x1 = [_dot_nt(bf(c["ah"]), s) + _dot(bf(a), v) for c, s, a, v in zip(chains, spb, a_ak, vb)]
    u = [t + _dot(bf(xi), blk(t)) for t, xi in zip(x1, x)]
    y = [_dot_nt(bf(c["rh"]), s) + _dot(bf(a), cat([blk(ui), v]))
         for c, s, a, ui, v in zip(chains, spb, a_r, u, vb)]
    upd = [_dot_tn(cat([bf(ui), bf(c["v"])]), cat([bf(c["bb"]), bf(c["kb"])])) for c, ui in zip(chains, u)]
    sp_new = [c["sp"] * c["dtot"] + jnp.where(bdmask, t, 0.0) for c, t in zip(chains, upd)]
    return list(zip(y, sp_new))


def _split3(t):
    hi = t.astype(BF16)
    r1 = t - hi.astype(F32)
    mid = r1.astype(BF16)
    lo = (r1 - mid.astype(F32)).astype(BF16)
    return hi, mid, lo


def _delta_prep(r, lw, k, v, kk, b, rev):
    C = CHUNK

    def causal(width, strict):
        ti = lax.broadcasted_iota(jnp.int32, (C, width), 0)
        ii = lax.broadcasted_iota(jnp.int32, (C, width), 1) & (C - 1)
        if rev:
            return ii > ti if strict else ii >= ti
        return ii < ti if strict else ii <= ti

    tri = causal(C, False).astype(BF16)
    L = sum(_dot(tri, part) for part in _split3(lw))
    last = 0 if rev else C - 1
    ltot = L[last:last + 1, :]
    e_nl = jnp.exp(-L)
    e_t = jnp.exp(ltot - L)
    return dict(ah=-kk * jnp.exp(L - lw), rh=r * jnp.exp(L), bh=b * e_nl, kh=k * e_nl, bb=b * e_t,
                kb=k * e_t, v=v, dtot=jnp.exp(ltot)), causal(2 * C, True), causal(4 * C, False)


def _delta_body(*refs, nb):
    ins, outs, s_ref = refs[:12 * nb], refs[12 * nb:14 * nb], refs[14 * nb]

    @pl.when(pl.program_id(1) == 0)
    def _():
        s_ref[...] = jnp.zeros_like(s_ref)

    C, W = ins[0].shape
    m_a = lax.broadcasted_iota(jnp.int32, (C, LANES), 1) < (LANES // 2)
    br = lax.broadcasted_iota(jnp.int32, (LANES, LANES), 0) < (LANES // 2)
    bc = lax.broadcasted_iota(jnp.int32, (LANES, LANES), 1) < (LANES // 2)
    bdmask = br == bc
    chains, dest = [], []
    for d in range(N_DIR):
        for j in range(nb):
            r_ref, v_ref, kk_ref, lw_ref, k_ref, b_ref = ins[(d * nb + j) * 6:(d * nb + j + 1) * 6]
            ops, strict2, incl4 = _delta_prep(r_ref[...], lw_ref[...], k_ref[...], v_ref[...], kk_ref[...],
                                              b_ref[...], rev=(d == 1))
            for p in range(W // LANES):
                sl = slice(p * LANES, (p + 1) * LANES)
                c = {name: t[:, sl] for name, t in ops.items()}
                c.update(sp=s_ref[d, j, p], strict2=strict2, incl4=incl4)
                chains.append(c)
                dest.append((outs[d * nb + j], d, j, p, sl))
    for (y, sp_new), (y_ref, d, j, p, sl) in zip(_delta_chains(chains, m_a, bdmask), dest):
        y_ref[:, sl] = y
        s_ref[d, j, p] = sp_new


def rwkv_scan(r, v, kk, lw, k, b, n_samples, n_lat, n_ctx, samples=2):
    M, W = r.shape
    ncl, ncc = n_lat // CHUNK, n_ctx // CHUNK
    nc = ncl + ncc
    nb = _pick(n_samples, samples)
    ng = n_samples // nb

    def row_block(n_s, s, t):
        return jnp.where(t < ncc, n_s * ncl + s * ncc + t, s * ncl + (t - ncc))

    def step_chunk(d, c):
        if d == 0:
            return c
        return jnp.where(c < ncc, ncc - 1 - c, nc + ncc - 1 - c)

    in_specs, out_specs, args = [], [], []
    for d in range(N_DIR):
        for j in range(nb):
            def rows(bi, c, d=d, j=j):
                return row_block(n_samples, j * ng + bi, step_chunk(d, c))

            def out_rows(bi, c, d=d):
                return row_block(ng, bi, step_chunk(d, c))
            shared = pl.BlockSpec((CHUNK, W), lambda bi, c, rows=rows: (rows(bi, c), 0))
            per_dir = pl.BlockSpec((None, CHUNK, W), lambda bi, c, rows=rows, d=d: (d, rows(bi, c), 0))
            in_specs += [shared] * 3 + [per_dir] * 3
            args += [r, v, kk, lw, k, b]
            out_specs.append(pl.BlockSpec((CHUNK, W), lambda bi, c, out_rows=out_rows: (out_rows(bi, c), 0)))
    outs = pl.pallas_call(
        functools.partial(_delta_body, nb=nb),
        grid=(ng, nc),
        in_specs=in_specs,
        out_specs=out_specs,
        out_shape=[jax.ShapeDtypeStruct((M // nb, W), F32)] * (N_DIR * nb),
        scratch_shapes=[pltpu.VMEM((N_DIR, nb, W // LANES, LANES, LANES), F32)],
        compiler_params=_params("parallel", "arbitrary"),
        name="delta_scan",
    )(*args)
    return outs


VT_ONES = 16
ATTN_LOOKAHEAD = 6


def _attn_body(*refs, n_seg, kv_chunk):
    q_ref, kv, o_ref = refs[0], refs[1:1 + 2 * n_seg], refs[1 + 2 * n_seg]
    items = []
    for g in range(n_seg):
        n = kv[2 * g].shape[0]
        ck = min(kv_chunk, n)
        items += [(hh, g, c0, ck) for c0 in range(0, n, ck) for hh in range(2)]
    q = [q_ref[:, hh * QK_PAD:(hh + 1) * QK_PAD] for hh in range(2)]
    m = [None, None]
    acc = [None, None]
    scores = {}
    for idx in range(len(items) + ATTN_LOOKAHEAD):
        if idx < len(items):
            hh, g, c0, ck = items[idx]
            scores[idx] = _dot_nt(kv[2 * g][c0:c0 + ck, hh * QK_PAD:(hh + 1) * QK_PAD], q[hh])
        if idx >= ATTN_LOOKAHEAD:
            hh, g, c0, ck = items[idx - ATTN_LOOKAHEAD]
            st = scores.pop(idx - ATTN_LOOKAHEAD)
            mc = jnp.max(st, axis=0, keepdims=True)
            m_new = mc if m[hh] is None else jnp.maximum(m[hh], mc)
            pt = jnp.exp2(st - m_new).astype(BF16)
            v0 = hh * QK_PAD + QK_PAD - V_HEAD - VT_ONES
            pv = _dot(kv[2 * g + 1][v0:v0 + VT_ONES + V_HEAD, c0:c0 + ck], pt)
            acc[hh] = pv if m[hh] is None else acc[hh] * jnp.exp2(m[hh] - m_new) + pv
            m[hh] = m_new
    outs = [a[VT_ONES:] / a[0:1] for a in acc]
    o_ref[...] = jnp.concatenate(outs, axis=0).T.astype(o_ref.dtype)


def attention(q, k, vt, n_samples, q_row0, n_q, segments, tq=256, kv_chunk=256):
    H = q.shape[1] // QK_PAD
    tq = _pick(n_q, tq)
    assert q_row0 % tq == 0
    nq = n_q // tq
    in_specs = [pl.BlockSpec((tq, 2 * QK_PAD), lambda b, h, i: (q_row0 // tq + b * nq + i, h))]
    args = [q]
    for row0, n in segments:
        assert row0 % n == 0
        in_specs += [pl.BlockSpec((n, 2 * QK_PAD), lambda b, h, i, row0=row0, n=n: (row0 // n + b, h)),
                     pl.BlockSpec((2 * QK_PAD, n), lambda b, h, i, row0=row0, n=n: (h, row0 // n + b))]
        args += [k, vt]
    return pl.pallas_call(
        functools.partial(_attn_body, n_seg=len(segments), kv_chunk=kv_chunk),
        grid=(n_samples, H // 2, nq),
        in_specs=in_specs,
        out_specs=pl.BlockSpec((tq, 2 * V_HEAD), lambda b, h, i: (b * nq + i, h)),
        out_shape=jax.ShapeDtypeStruct((n_samples * n_q, H * V_HEAD), BF16),
        compiler_params=_params("parallel", "parallel", "parallel"),
        name="attention",
    )(*args)


def _lane_sum(t, ones):
    tb = t.astype(BF16)
    return jnp.concatenate([_dot(tb[:, c:c + LANES], ones) for c in range(0, t.shape[1], LANES)], axis=1)


def _mla_body(z_ref, gcq_ref, gckv_ref, wq_ref, wk_ref, wv_ref, gq_ref, gk_ref, cos_ref, sin_ref, ones_ref,
              q_ref, k_ref, vt_ref, *, q_scale):
    tm = z_ref.shape[0]
    H = MLA_HEADS
    ones = ones_ref[...]
    c_q = z_ref[:, :Q_LORA]
    c_kv = z_ref[:, Q_LORA:Q_LORA + KV_LORA]
    cq_n = c_q * lax.rsqrt(jnp.mean(c_q * c_q, axis=-1, keepdims=True) + RMS_EPS) * gcq_ref[...]
    ckv_n = (c_kv * lax.rsqrt(jnp.mean(c_kv * c_kv, axis=-1, keepdims=True) + RMS_EPS)
             * gckv_ref[...]).astype(BF16)
    lane = lax.broadcasted_iota(jnp.int32, (tm, LANES), 1)
    k_rope = pltpu.roll(jnp.where(lane < QK_ROPE, z_ref[:, Q_LORA + KV_LORA:Q_LORA + KV_LORA + LANES], 0.0),
                        QK_NOPE, 1)
    q = _dot(cq_n.astype(BF16), wq_ref[...])
    k = _dot(ckv_n, wk_ref[...]) + jnp.concatenate([k_rope] * H, axis=1)
    cos = jnp.concatenate([cos_ref[...]] * H, axis=1)
    sin = jnp.concatenate([sin_ref[...]] * H, axis=1)
    first_half = (lax.broadcasted_iota(jnp.int32, (tm, H * LANES), 1) & (ROPE_AXIS // 2)) == 0

    def finish(t, gain, scale):
        t = t * lax.rsqrt(_lane_sum(t * t, ones) * (1.0 / QK_HEAD) + RMS_EPS) * gain
        partner = jnp.where(first_half, pltpu.roll(t, H * LANES - ROPE_AXIS // 2, 1),
                            pltpu.roll(t, ROPE_AXIS // 2, 1))
        return ((t * cos + partner * sin) * scale).astype(BF16)

    q_ref[...] = finish(q, gq_ref[...], q_scale)
    k_ref[...] = finish(k, gk_ref[...], 1.0)
    v = _dot(ckv_n, wv_ref[...])
    ones_rows = (lax.broadcasted_iota(jnp.int32, (tm, H * LANES), 1) & (LANES - 1)) < QK_PAD - V_HEAD
    vt_ref[...] = jnp.where(ones_rows, 1.0, v).T.astype(BF16)


def mla_prep(zm, p, rope_cos, rope_sin, n_lat_rows, n_lat, tm=256):
    M, ZW = zm.shape
    H = MLA_HEADS
    src = jnp.arange(H * QK_HEAD, dtype=jnp.int32)[:, None]
    dst = jnp.arange(H * QK_PAD, dtype=jnp.int32)[None, :]
    spread = ((src // QK_HEAD == dst // QK_PAD) & (src % QK_HEAD == dst % QK_PAD)).astype(BF16)
    w_uq = matmul_bf16(p["w_uq"].astype(BF16), spread)
    is_k = (jnp.arange(H * QK_PAD, dtype=jnp.int32) % QK_PAD) < QK_NOPE
    w_uk = jnp.where(is_k, p["w_ukv"], 0.0).astype(BF16)
    w_uv = jnp.where(is_k, 0.0, p["w_ukv"]).astype(BF16)
    gain = lambda g: jnp.tile(jnp.pad(g, (0, QK_PAD - QK_HEAD)), H).reshape(1, H * QK_PAD)
    ones = jnp.ones((LANES, LANES), BF16)
    consts = [p["g_cq"].reshape(1, Q_LORA), p["g_ckv"].reshape(1, KV_LORA), w_uq, w_uk, w_uv,
              gain(p["g_qn"]), gain(p["g_kn"])]
    row = lambda i: (i, 0)
    whole = lambda i: (0, 0)
    full = lambda t: pl.BlockSpec(t.shape, whole)
    lat_tiles, per_sample = n_lat_rows // tm, n_lat // tm
    tab = lambda i: (jnp.where(i < lat_tiles, i % per_sample, per_sample), 0)
    return pl.pallas_call(
        functools.partial(_mla_body, q_scale=QK_HEAD ** -0.5 * math.log2(math.e)),
        grid=(M // tm,),
        in_specs=[pl.BlockSpec((tm, ZW), row)] + [full(t) for t in consts]
                 + [pl.BlockSpec((tm, LANES), tab), pl.BlockSpec((tm, LANES), tab), full(ones)],
        out_specs=[pl.BlockSpec((tm, H * QK_PAD), row), pl.BlockSpec((tm, H * QK_PAD), row),
                   pl.BlockSpec((H * QK_PAD, tm), lambda i: (0, i))],
        out_shape=[jax.ShapeDtypeStruct((M, H * QK_PAD), BF16), jax.ShapeDtypeStruct((M, H * QK_PAD), BF16),
                   jax.ShapeDtypeStruct((H * QK_PAD, M), BF16)],
        compiler_params=_params("parallel"),
        name="mla_prep",
    )(zm, *consts, rope_cos, rope_sin, ones)


def _rope_lane_tables(n_tokens, tile):
    ang_row, ang_col = _rope_tables(n_tokens)
    one = jnp.ones((n_tokens, QK_NOPE), F32)
    zero = jnp.zeros((n_tokens, QK_NOPE), F32)
    pad1 = jnp.ones((n_tokens, QK_PAD - QK_HEAD), F32)
    pad0 = jnp.zeros((n_tokens, QK_PAD - QK_HEAD), F32)
    cr, cc, sr, sc = jnp.cos(ang_row), jnp.cos(ang_col), jnp.sin(ang_row), jnp.sin(ang_col)
    cos = jnp.concatenate([one, cr, cr, cc, cc, pad1], axis=1)
    sin = jnp.concatenate([zero, -sr, sr, -sc, sc, pad0], axis=1)
    cos = jnp.concatenate([cos, jnp.ones((tile, QK_PAD), F32)], axis=0)
    sin = jnp.concatenate([sin, jnp.zeros((tile, QK_PAD), F32)], axis=0)
    return cos, sin


def _rope_tables(n_tokens):
    rows = n_tokens // GRID_W
    row = jnp.broadcast_to(jnp.arange(rows)[:, None], (rows, GRID_W)).reshape(-1).astype(F32)
    col = jnp.broadcast_to(jnp.arange(GRID_W)[None, :], (rows, GRID_W)).reshape(-1).astype(F32)
    inv_freq = ROPE_THETA ** (-jnp.arange(0, ROPE_AXIS, 2, dtype=F32) / ROPE_AXIS)
    return row[:, None] * inv_freq, col[:, None] * inv_freq


def _dft_time_matrix(n):
    f = int(round(math.sqrt(n)))
    assert f * f == n
    k = jnp.arange(n, dtype=jnp.int32)[:, None]
    t = jnp.arange(f, dtype=jnp.int32)[None, :]
    a = ((k * t) % f).astype(F32) * (2.0 * math.pi / f)
    b = ((k * t) % n).astype(F32) * (2.0 * math.pi / n)
    ca, sa = jnp.cos(a)[:, :, None], jnp.sin(a)[:, :, None]
    cb, sb = jnp.cos(b)[:, None, :], jnp.sin(b)[:, None, :]
    cos = (ca * cb - sa * sb).reshape(n, n)
    sin = (sa * cb + ca * sb).reshape(n, n)
    return jnp.concatenate([cos, -sin], axis=1).astype(BF16)


def _dft_channel_matrix():
    idx = jnp.arange(FNO_GROUP_DIM, dtype=jnp.int32)
    ang = ((idx[:, None] * idx[None, :]) % FNO_GROUP_DIM).astype(F32) * (2.0 * math.pi / FNO_GROUP_DIM)
    eye = jnp.eye(FNO_GROUPS, dtype=F32)
    return jnp.concatenate([jnp.kron(eye, jnp.cos(ang)), jnp.kron(eye, jnp.sin(ang))], axis=1)


def _layer(xg, c_act, cctx_act, p, last, consts, dims):
    B, S, Sc = dims
    M, D = xg.shape
    ML = B * S
    dft_x, dft_c, dft_ch, (rope_cos, rope_sin), head_ones = consts

    act = jnp.concatenate([c_act, cctx_act[None, :], jnp.zeros((7 - B, D), F32)], axis=0)
    modtab = matmul_f32_bias(act, p["w_mod"], p["b_mod"])[:B + 1].reshape(B + 1, 6, D)

    w_in = p["w_in"]
    o1 = N_RWKV_IN
    o2 = o1 + N_MLA_IN
    o3 = o2 + FNO_WIDTH
    w_fd = matmul_f32_bias(w_in[:, o2:o3], dft_ch, jnp.zeros((2 * FNO_WIDTH,), F32))
    zr, zm, zg, zd = inproj(xg, modtab, p["g_norm1"], w_in[:, :ZR_PAD].astype(BF16),
                            w_in[:, o1:o1 + ZM_PAD].astype(BF16), w_in[:, o3:].astype(BF16),
                            w_fd.astype(BF16), S)

    r, v, kkf, lw_dir, k_dir, b_dir, g_out, bonus = rwkv_features(zr, p, head_ones, B, S, Sc)
    y_slots = rwkv_scan(r, v, kkf, lw_dir, k_dir, b_dir, B, S, Sc)

    qh, kh, vh = mla_prep(zm, p, rope_cos, rope_sin, ML, S)
    att = attention(qh, kh, vh, B, 0, S, [(0, S), (ML, Sc)])

    yf = time_dft(dft_x, zd, B, 0, 1.0 / math.sqrt(S * FNO_GROUP_DIM))

    n_rows = ML if last else M
    if not last:
        att = (att, attention(qh, kh, vh, B, ML, Sc, [(ML, Sc)]))
        yf = (yf, time_dft(dft_c, zd, B, ML, 1.0 / math.sqrt(Sc * FNO_GROUP_DIM)))
    else:
        att, yf = (att, att), (yf, yf)
    x_mid = post(y_slots, bonus, g_out, p["gn_w"], p["gn_b"], head_ones, att, yf, zg, xg, modtab,
                 p["w_rwkv_o"], p["w_mla_o"], p["w_fno"], p["w_out"], n_rows, S, Sc)
    return ffn(x_mid, modtab, p["g_norm2"], p["w_ffn_in"], p["w_ffn_out"], S)


def kernel(x, c, ctx, c_ctx, w_mod, b_mod, g_norm1, w_in, mu_prev, mu_next, w0, w2, a0, a2, k_k, k_a, r_k, g2, gn_w, gn_b, w_rwkv_o, g_cq, g_ckv, w_uq, w_ukv, g_qn, g_kn, w_mla_o, w_fno, w_out, g_norm2, w_ffn_in, w_ffn_out):
    layer_params = dict(w_mod=w_mod, b_mod=b_mod, g_norm1=g_norm1, w_in=w_in, mu_prev=mu_prev,
                        mu_next=mu_next, w0=w0, w2=w2, a0=a0, a2=a2, k_k=k_k, k_a=k_a, r_k=r_k,
                        g2=g2, gn_w=gn_w, gn_b=gn_b, w_rwkv_o=w_rwkv_o, g_cq=g_cq, g_ckv=g_ckv,
                        w_uq=w_uq, w_ukv=w_ukv, g_qn=g_qn, g_kn=g_kn, w_mla_o=w_mla_o, w_fno=w_fno,
                        w_out=w_out, g_norm2=g_norm2, w_ffn_in=w_ffn_in, w_ffn_out=w_ffn_out)
    depth = w_mod.shape[0]
    B, S, D = x.shape
    Sc = ctx.shape[1]
    assert S % ROW_TILE == 0 and (B * Sc) % ROW_TILE == 0 and B <= 7
    head_ones = jnp.kron(jnp.eye(RWKV_HEADS, dtype=F32),
                         jnp.ones((RWKV_HEAD_DIM, RWKV_HEAD_DIM), F32)).astype(BF16)
    consts = (_dft_time_matrix(S), _dft_time_matrix(Sc), _dft_channel_matrix(),
              _rope_lane_tables(S, 256), head_ones)
    c_act = jax.nn.silu(c)
    cctx_act = jax.nn.silu(c_ctx)
    xg = jnp.concatenate([x.reshape(B * S, D), ctx.reshape(B * Sc, D)], axis=0)
    for layer in range(depth):
        p = {name: arr[layer] for name, arr in layer_params.items()}
        xg = _layer(xg, c_act, cctx_act, p, layer == depth - 1, consts, (B, S, Sc))
    return xg[:B * S].reshape(B, S, D)
```
